```python
import jax, jax.numpy as jnp
from jax import lax
import numpy as np

D_MODEL = 2048
BATCH = 16
SEQ = 256
DEPTH = 1
DEC_BATCH = 8
DEC_SEQ = 4096
PAST_LEN = 512

GRID_W = 64
N_HEADS = 16
N_KV_HEADS = 4
HEAD_DIM = 128
Q_GROUP = N_HEADS // N_KV_HEADS
D_Q = N_HEADS * HEAD_DIM
D_KV = N_KV_HEADS * HEAD_DIM
ROPE_HALF = HEAD_DIM // 2
ROPE_PAIRS_PER_AXIS = HEAD_DIM // 4
ROPE_THETA = 10000.0
Q_BLOCK = 128
POOL_WINDOWS = (2, 4, 8, 16)
N_POOL_GROUPS = len(POOL_WINDOWS)
POOL_GROUP_DIM = D_MODEL // 8
D_POOL = N_POOL_GROUPS * POOL_GROUP_DIM
D_IN = D_POOL + D_Q + 2 * D_KV
N_EXPERTS = 32
TOP_K = 4
D_FF = D_MODEL
SWIGLU_LIMIT = 7.0
SWIGLU_ALPHA = 1.702
MOE_BLOCK = 128
N_MOD = 6
EPS = 1e-6

kernel_name = 'hybrid_pool_gqa_moe_diffusion_step'


def rms_norm(x, g):
    xf = x.astype(jnp.float32)
    y = xf * lax.rsqrt(jnp.mean(xf * xf, axis=-1, keepdims=True) + EPS)
    return (y * g.astype(jnp.float32)).astype(x.dtype)


def axial_rope_tables(rows):
    row = jnp.repeat(jnp.arange(rows), GRID_W).astype(jnp.float32)
    col = jnp.tile(jnp.arange(GRID_W), rows).astype(jnp.float32)
    freqs = ROPE_THETA ** (-jnp.arange(ROPE_PAIRS_PER_AXIS, dtype=jnp.float32) / ROPE_PAIRS_PER_AXIS)
    ang = jnp.concatenate([row[:, None] * freqs, col[:, None] * freqs], axis=-1)
    return jnp.cos(ang), jnp.sin(ang)


def apply_axial_rope(x, cos, sin):
    xf = x.astype(jnp.float32)
    x1, x2 = xf[..., :ROPE_HALF], xf[..., ROPE_HALF:]
    c = cos[None, :, None, :]
    s = sin[None, :, None, :]
    return jnp.concatenate([x1 * c - x2 * s, x2 * c + x1 * s], axis=-1).astype(x.dtype)


def blocked_attention(q, k, v):
    b, s = q.shape[0], q.shape[1]
    n_blk = s // Q_BLOCK
    scale = HEAD_DIM ** -0.5
    qb = q.reshape(b, n_blk, Q_BLOCK, N_KV_HEADS, Q_GROUP, HEAD_DIM).transpose(1, 0, 2, 3, 4, 5)

    def one_block(q_blk):
        scores = jnp.einsum('bqkgd,blkd->bkgql', q_blk, k).astype(jnp.float32) * scale
        p = jax.nn.softmax(scores, axis=-1)
        return jnp.einsum('bkgql,blkd->bqkgd', p.astype(v.dtype), v)

    o = lax.map(one_block, qb)
    return o.transpose(1, 0, 2, 3, 4, 5).reshape(b, s, D_Q)


def pool_mixer(u, w_pool, pool_scale):
    s = u.shape[1]
    uf = u.astype(jnp.float32)
    cs = jnp.concatenate([jnp.zeros_like(uf[:, :1]), jnp.cumsum(uf, axis=1)], axis=1)
    t = jnp.arange(s)
    outs = []
    for g, w in enumerate(POOL_WINDOWS):
        lo = jnp.clip(t - w // 2, 0, s)
        hi = jnp.clip(t + w - w // 2, 0, s)
        sl = slice(g * POOL_GROUP_DIM, (g + 1) * POOL_GROUP_DIM)
        csg = cs[..., sl]
        mean = (csg[:, hi] - csg[:, lo]) / (hi - lo).astype(jnp.float32)[None, :, None]
        pooled = (mean - uf[..., sl]).astype(u.dtype)
        outs.append(jnp.einsum('bsc,ce->bse', pooled, w_pool[g]))
    return jnp.concatenate(outs, axis=-1) * pool_scale


def token_mixer(h, rope, k_ctx, v_ctx, w_in, q_norm_g, k_norm_g, w_pool, pool_scale,
                w_a_out, w_b_out, w_gate, b_gate, w_o):
    b, s, _ = h.shape
    proj = jnp.einsum('bsd,de->bse', h, w_in)
    u_pool = proj[..., :D_POOL]
    q = proj[..., D_POOL:D_POOL + D_Q].reshape(b, s, N_HEADS, HEAD_DIM)
    k = proj[..., D_POOL + D_Q:D_POOL + D_Q + D_KV].reshape(b, s, N_KV_HEADS, HEAD_DIM)
    v = proj[..., D_POOL + D_Q + D_KV:].reshape(b, s, N_KV_HEADS, HEAD_DIM)
    q = rms_norm(q, q_norm_g)
    k = rms_norm(k, k_norm_g)
    if rope is not None:
        cos, sin = rope
        q = apply_axial_rope(q, cos, sin)
        k = apply_axial_rope(k, cos, sin)
    if k_ctx is None:
        k_all, v_all = k, v
    else:
        k_all = jnp.concatenate([k_ctx.astype(k.dtype), k], axis=1)
        v_all = jnp.concatenate([v_ctx.astype(v.dtype), v], axis=1)
    attn = blocked_attention(q, k_all, v_all)
    y_a = jnp.einsum('bsc,cd->bsd', pool_mixer(u_pool, w_pool, pool_scale), w_a_out)
    y_b = jnp.einsum('bsc,cd->bsd', attn, w_b_out)
    gates = jax.nn.sigmoid(jnp.einsum('bsd,de->bse', h, w_gate) + b_gate)
    merged = gates[..., :D_MODEL] * y_a + gates[..., D_MODEL:] * y_b
    return jnp.einsum('bsd,de->bse', merged, w_o), k, v


def clamped_swiglu(gu):
    g, lin = gu[..., :D_FF], gu[..., D_FF:]
    g = jnp.minimum(g, SWIGLU_LIMIT)
    lin = jnp.clip(lin, -SWIGLU_LIMIT, SWIGLU_LIMIT)
    return g * jax.nn.sigmoid(SWIGLU_ALPHA * g) * (lin + 1.0)


def moe(h, w_router, b_router, w_gu, b_gu, w_down, b_down):
    b, s, d = h.shape
    x = h.reshape(-1, d)
    n = x.shape[0]
    logits = jnp.einsum('nd,de->ne', x, w_router).astype(jnp.float32) + b_router.astype(jnp.float32)
    top_v, top_i = lax.top_k(logits, TOP_K)
    gate = jax.nn.softmax(top_v, axis=-1)
    nk = n * TOP_K
    flat_e = top_i.reshape(-1)
    order = jnp.argsort(flat_e)
    sorted_e = flat_e[order]
    counts = jnp.bincount(flat_e, length=N_EXPERTS)
    padded = (counts + MOE_BLOCK - 1) // MOE_BLOCK * MOE_BLOCK
    pad_end = jnp.cumsum(padded)
    pad_start = pad_end - padded
    raw_start = jnp.cumsum(counts) - counts
    dest = pad_start[sorted_e] + jnp.arange(nk) - raw_start[sorted_e]
    n_blocks = -(-nk // MOE_BLOCK) + N_EXPERTS
    n_slots = n_blocks * MOE_BLOCK
    slot_tok = jnp.full((n_slots,), n, jnp.int32).at[dest].set((order // TOP_K).astype(jnp.int32))
    slot_w = jnp.zeros((n_slots,), jnp.float32).at[dest].set(gate.reshape(-1)[order])
    block_e = jnp.minimum(jnp.searchsorted(pad_end, jnp.arange(n_blocks) * MOE_BLOCK, side='right'),
                          N_EXPERTS - 1)
    x_pad = jnp.concatenate([x, jnp.zeros((1, d), x.dtype)], axis=0)

    def run_block(args):
        tok, e = args
        xb = x_pad[tok]
        gu = xb @ w_gu[e] + b_gu[e]
        return clamped_swiglu(gu) @ w_down[e] + b_down[e]

    y_slots = lax.map(run_block, (slot_tok.reshape(n_blocks, MOE_BLOCK), block_e))
    y_slots = (y_slots.reshape(n_slots, d).astype(jnp.float32) * slot_w[:, None]).astype(x.dtype)
    out = jnp.zeros((n + 1, d), x.dtype).at[slot_tok].add(y_slots)[:n]
    return out.reshape(b, s, d)


def layer_forward(x, mod, rope, k_ctx, v_ctx, norm1_g, norm2_g, mix_w, moe_w):
    shift1, scale1, gate1, shift2, scale2, gate2 = jnp.split(mod, N_MOD, axis=-1)
    h = rms_norm(x, norm1_g) * (1.0 + scale1) + shift1
    mix, k, v = token_mixer(h, rope, k_ctx, v_ctx, *mix_w)
    x = x + gate1 * mix
    h = rms_norm(x, norm2_g) * (1.0 + scale2) + shift2
    x = x + gate2 * moe(h, *moe_w)
    return x, k, v


def setup_inputs(seed: int = 0) -> dict:
    key = jax.random.key(seed)
    ks = iter(jax.random.split(key, 40))

    def nrm(shape, scale):
        return jax.random.normal(next(ks), shape, jnp.float32) * scale

    d = D_MODEL
    return {
        'x_prompt': nrm((BATCH, SEQ, d), 1.0),
        'x_sample': nrm((DEC_BATCH, DEC_SEQ, d), 1.0),
        'cache_k': nrm((DEC_BATCH, DEPTH, PAST_LEN, N_KV_HEADS, HEAD_DIM), 1.0),
        'cache_v': nrm((DEC_BATCH, DEPTH, PAST_LEN, N_KV_HEADS, HEAD_DIM), 1.0),
        'c': nrm((DEC_BATCH, d), 1.0),
        'c_ctx': nrm((d,), 1.0),
        'w_mod': nrm((DEPTH, d, N_MOD * d), 0.5 * d ** -0.5),
        'b_mod': nrm((DEPTH, N_MOD * d), 0.02),
        'norm1_g': 1.0 + nrm((DEPTH, d), 0.02),
        'w_in': nrm((DEPTH, d, D_IN), d ** -0.5),
        'q_norm_g': 1.0 + nrm((DEPTH, HEAD_DIM), 0.02),
        'k_norm_g': 1.0 + nrm((DEPTH, HEAD_DIM), 0.02),
        'w_pool': nrm((DEPTH, N_POOL_GROUPS, POOL_GROUP_DIM, POOL_GROUP_DIM), POOL_GROUP_DIM ** -0.5),
        'pool_scale': 1.0 + nrm((DEPTH, D_POOL), 0.02),
        'w_a_out': nrm((DEPTH, D_POOL, d), D_POOL ** -0.5),
        'w_b_out': nrm((DEPTH, D_Q, d), D_Q ** -0.5),
        'w_gate': nrm((DEPTH, d, 2 * d), d ** -0.5),
        'b_gate': nrm((DEPTH, 2 * d), 0.02),
        'w_o': nrm((DEPTH, d, d), d ** -0.5),
        'norm2_g': 1.0 + nrm((DEPTH, d), 0.02),
        'w_router': nrm((DEPTH, d, N_EXPERTS), d ** -0.5),
        'b_router': nrm((DEPTH, N_EXPERTS), 0.01),
        'w_gu': nrm((DEPTH, N_EXPERTS, d, 2 * D_FF), d ** -0.5),
        'b_gu': nrm((DEPTH, N_EXPERTS, 2 * D_FF), 0.02),
        'w_down': nrm((DEPTH, N_EXPERTS, D_FF, d), D_FF ** -0.5),
        'b_down': nrm((DEPTH, N_EXPERTS, d), 0.02),
        'final_g': 1.0 + nrm((d,), 0.02),
    }


def reference(x_prompt, x_sample, cache_k, cache_v, c, c_ctx, w_mod, b_mod, norm1_g, w_in,
              q_norm_g, k_norm_g, w_pool, pool_scale, w_a_out, w_b_out, w_gate, b_gate, w_o,
              norm2_g, w_router, b_router, w_gu, b_gu, w_down, b_down, final_g):
    rows = x_sample.shape[1] // GRID_W
    rope = axial_rope_tables(rows)
    xp, xs = x_prompt, x_sample
    new_k, new_v = [], []
    for l in range(DEPTH):
        mix_w = (w_in[l], q_norm_g[l], k_norm_g[l], w_pool[l], pool_scale[l], w_a_out[l],
                 w_b_out[l], w_gate[l], b_gate[l], w_o[l])
        moe_w = (w_router[l], b_router[l], w_gu[l], b_gu[l], w_down[l], b_down[l])
        mod_ctx = (jax.nn.silu(c_ctx) @ w_mod[l] + b_mod[l])[None, None, :]
        mod_lat = (jax.nn.silu(c) @ w_mod[l] + b_mod[l])[:, None, :]
        xp, k_ctx, v_ctx = layer_forward(xp, mod_ctx, None, None, None, norm1_g[l], norm2_g[l],
                                         mix_w, moe_w)
        new_k.append(k_ctx)
        new_v.append(v_ctx)
        xs, _, _ = layer_forward(xs, mod_lat, rope, cache_k[:, l], cache_v[:, l], norm1_g[l],
                                 norm2_g[l], mix_w, moe_w)
    y_prompt = rms_norm(xp, final_g)
    y_sample = rms_norm(xs, final_g)
    new_cache_k = jnp.stack(new_k, axis=1)
    new_cache_v = jnp.stack(new_v, axis=1)
    return (y_prompt, y_sample, new_cache_k, new_cache_v)
```

```python
import functools

import jax
import jax.numpy as jnp
from jax import lax
from jax.experimental import pallas as pl
from jax.experimental.pallas import tpu as pltpu

F32 = jnp.float32
BF16 = jnp.bfloat16
I32 = jnp.int32

GRID_W = 64
N_HEADS = 16
N_KV_HEADS = 4
HEAD_DIM = 128
Q_GROUP = N_HEADS // N_KV_HEADS
ROPE_PAIRS_PER_AXIS = HEAD_DIM // 4
ROPE_THETA = 10000.0
POOL_WINDOWS = (2, 4, 8, 16)
TOP_K = 4
SWIGLU_LIMIT = 7.0
SWIGLU_ALPHA = 1.702
N_MOD = 6
EPS = 1e-6

LANES = 128
SUBLANES = 8
VMEM_LIMIT_BYTES = 56 * 1024 * 1024

EXPERT_TILE_ROWS = 512
POOL_HALO = SUBLANES


def _cparams(semantics):
    return pltpu.CompilerParams(dimension_semantics=semantics, vmem_limit_bytes=VMEM_LIMIT_BYTES)


def _pick(n, candidates):
    for c in candidates:
        if n % c == 0:
            return c
    raise ValueError(f"no tile in {candidates} divides {n}")


def _resident(shape, index_map):
    return pl.BlockSpec(shape, index_map, pipeline_mode=pl.Buffered(1))


def _mod_kernel(c_ref, w_ref, b_ref, o_ref):
    c = c_ref[...]
    s = (c * jax.nn.sigmoid(c)).astype(BF16)
    o_ref[...] = jnp.dot(s, w_ref[...].astype(BF16), preferred_element_type=F32) + b_ref[...]


def _modulation(cvec, w_mod, b_mod):
    r, d = cvec.shape
    n6 = w_mod.shape[1]
    tn = _pick(n6, (1024, 512, 256, 128))
    return pl.pallas_call(
        _mod_kernel,
        out_shape=jax.ShapeDtypeStruct((r, n6), F32),
        grid=(n6 // tn,),
        in_specs=[
            pl.BlockSpec((r, d), lambda j: (0, 0)),
            pl.BlockSpec((d, tn), lambda j: (0, j)),
            pl.BlockSpec((1, tn), lambda j: (0, j)),
        ],
        out_specs=pl.BlockSpec((r, tn), lambda j: (0, j)),
        compiler_params=_cparams(("arbitrary",)),
        name="modulation",
    )(cvec, w_mod, b_mod.reshape(1, n6))


def _inproj_kernel(*refs, use_rope, emit_kv_f32, d_pool):
    it = iter(refs)
    x_ref, mod_ref, g1_ref, w_ref, qg_ref, kg_ref = (next(it) for _ in range(6))
    cos_ref = sin_ref = None
    if use_rope:
        cos_ref, sin_ref = next(it), next(it)
    h_ref, u_ref, q_ref, k_ref, v_ref = (next(it) for _ in range(5))
    kf_ref = vf_ref = None
    if emit_kv_f32:
        kf_ref, vf_ref = next(it), next(it)

    x = x_ref[...]
    ms = jnp.mean(x * x, axis=-1, keepdims=True)
    xn = x * lax.rsqrt(ms + EPS) * g1_ref[...]
    shift = mod_ref[0, 0:1, :]
    scale = mod_ref[0, 1:2, :]
    hb = (xn * (1.0 + scale) + shift).astype(BF16)
    h_ref[...] = hb

    d_q = N_HEADS * HEAD_DIM
    d_kv = N_KV_HEADS * HEAD_DIM
    chunk = Q_GROUP * HEAD_DIM

    for c0 in range(0, d_pool, chunk):
        u_ref[:, c0:c0 + chunk] = jnp.dot(hb, w_ref[:, c0:c0 + chunk], preferred_element_type=F32)

    if use_rope:
        cos = cos_ref[...]
        sin = sin_ref[...]

    def head(xh, g):
        hm = jnp.mean(xh * xh, axis=-1, keepdims=True)
        y = xh * lax.rsqrt(hm + EPS) * g
        if use_rope:
            y = y * cos + pltpu.roll(y, HEAD_DIM // 2, axis=1) * sin
        return y

    qg = qg_ref[...]
    kg = kg_ref[...]
    for c0 in range(0, d_q, chunk):
        pr = jnp.dot(hb, w_ref[:, d_pool + c0:d_pool + c0 + chunk], preferred_element_type=F32)
        for j in range(chunk // HEAD_DIM):
            yh = head(pr[:, j * HEAD_DIM:(j + 1) * HEAD_DIM], qg)
            q_ref[:, c0 + j * HEAD_DIM:c0 + (j + 1) * HEAD_DIM] = yh.astype(BF16)

    pk = jnp.dot(hb, w_ref[:, d_pool + d_q:d_pool + d_q + d_kv], preferred_element_type=F32)
    for j in range(N_KV_HEADS):
        yh = head(pk[:, j * HEAD_DIM:(j + 1) * HEAD_DIM], kg)
        k_ref[:, j * HEAD_DIM:(j + 1) * HEAD_DIM] = yh.astype(BF16)
        if emit_kv_f32:
            kf_ref[:, j * HEAD_DIM:(j + 1) * HEAD_DIM] = yh

    pv = jnp.dot(hb, w_ref[:, d_pool + d_q + d_kv:], preferred_element_type=F32)
    v_ref[...] = pv.astype(BF16)
    if emit_kv_f32:
        vf_ref[...] = pv


def _in_projection(x2d, seq, mod3, mod_row, g1, w_in_b, qg, kg, rope, emit_kv_f32):
    n, d = x2d.shape
    d_in = w_in_b.shape[1]
    d_q = N_HEADS * HEAD_DIM
    d_kv = N_KV_HEADS * HEAD_DIM
    d_pool = d_in - d_q - 2 * d_kv
    ts = _pick(seq, (512, 256, 128))
    tiles_per_seq = seq // ts
    use_rope = rope is not None

    in_specs = [
        pl.BlockSpec((ts, d), lambda i: (i, 0)),
        pl.BlockSpec((1, N_MOD, d), lambda i: (mod_row(i * ts), 0, 0)),
        _resident((1, d), lambda i: (0, 0)),
        _resident((d, d_in), lambda i: (0, 0)),
        _resident((1, HEAD_DIM), lambda i: (0, 0)),
        _resident((1, HEAD_DIM), lambda i: (0, 0)),
    ]
    args = [x2d, mod3, g1.reshape(1, d), w_in_b, qg.reshape(1, HEAD_DIM), kg.reshape(1, HEAD_DIM)]
    if use_rope:
        in_specs += [pl.BlockSpec((ts, HEAD_DIM), lambda i: (i % tiles_per_seq, 0))] * 2
        args += list(rope)

    out_shape = [
        jax.ShapeDtypeStruct((n, d), BF16),
        jax.ShapeDtypeStruct((n, d_pool), F32),
        jax.ShapeDtypeStruct((n, d_q), BF16),
        jax.ShapeDtypeStruct((n, d_kv), BF16),
        jax.ShapeDtypeStruct((n, d_kv), BF16),
    ]
    out_specs = [
        pl.BlockSpec((ts, d), lambda i: (i, 0)),
        pl.BlockSpec((ts, d_pool), lambda i: (i, 0)),
        pl.BlockSpec((ts, d_q), lambda i: (i, 0)),
        pl.BlockSpec((ts, d_kv), lambda i: (i, 0)),
        pl.BlockSpec((ts, d_kv), lambda i: (i, 0)),
    ]
    if emit_kv_f32:
        out_shape += [jax.ShapeDtypeStruct((n, d_kv), F32)] * 2
        out_specs += [pl.BlockSpec((ts, d_kv), lambda i: (i, 0))] * 2

    return pl.pallas_call(
        functools.partial(_inproj_kernel, use_rope=use_rope, emit_kv_f32=emit_kv_f32, d_pool=d_pool),
        out_shape=out_shape,
        grid=(n // ts,),
        in_specs=in_specs,
        out_specs=out_specs,
        compiler_params=_cparams(("arbitrary",)),
        name="in_projection",
    )(*args)


def _attn_kernel(q_ref, k_ref, v_ref, o_ref, *, tk, n_chunks, scale):
    q = q_ref[0]
    tq = q.shape[0]
    qs = jnp.concatenate([q[:, j * HEAD_DIM:(j + 1) * HEAD_DIM] for j in range(Q_GROUP)], axis=0)
    rows = qs.shape[0]

    def body(c, carry):
        m, l, acc = carry
        off = pl.multiple_of(c * tk, tk)
        kc = k_ref[0, pl.ds(off, tk), :]
        vc = v_ref[0, pl.ds(off, tk), :]
        s = lax.dot_general(qs, kc, (((1,), (1,)), ((), ())), preferred_element_type=F32) * scale
        m_new = jnp.maximum(m, jnp.max(s, axis=-1, keepdims=True))
        alpha = jnp.exp(m - m_new)
        p = jnp.exp(s - m_new)
        l = alpha * l + jnp.sum(p, axis=-1, keepdims=True)
        acc = alpha * acc + jnp.dot(p.astype(BF16), vc, preferred_element_type=F32)
        return m_new, l, acc

    init = (jnp.full((rows, 1), -jnp.inf, F32), jnp.zeros((rows, 1), F32), jnp.zeros((rows, HEAD_DIM), F32))
    _, l, acc = lax.fori_loop(0, n_chunks, body, init)
    out = acc / l
    o_ref[0] = jnp.concatenate([out[j * tq:(j + 1) * tq] for j in range(Q_GROUP)], axis=1).astype(o_ref.dtype)


def _attention(q3, k3, v3):
    b, s, d_q = q3.shape
    kv_len = k3.shape[1]
    tq = _pick(s, (128,))
    tk = _pick(kv_len, (512, 256, 128))
    gw = Q_GROUP * HEAD_DIM
    return pl.pallas_call(
        functools.partial(_attn_kernel, tk=tk, n_chunks=kv_len // tk, scale=HEAD_DIM ** -0.5),
        out_shape=jax.ShapeDtypeStruct((b, s, d_q), BF16),
        grid=(b, N_KV_HEADS, s // tq),
        in_specs=[
            pl.BlockSpec((1, tq, gw), lambda bi, g, qi: (bi, qi, g)),
            pl.BlockSpec((1, kv_len, HEAD_DIM), lambda bi, g, qi: (bi, 0, g)),
            pl.BlockSpec((1, kv_len, HEAD_DIM), lambda bi, g, qi: (bi, 0, g)),
        ],
        out_specs=pl.BlockSpec((1, tq, gw), lambda bi, g, qi: (bi, qi, g)),
        compiler_params=_cparams(("arbitrary", "arbitrary", "arbitrary")),
        name="attention",
    )(q3, k3, v3)


def _pool_kernel(up_ref, u_ref, un_ref, wp_ref, ps_ref, o_ref, ext_ref, *, tp, seq):
    i = pl.program_id(0)
    pos0 = (i * tp) % seq
    has_prev = pos0 > 0
    has_next = pos0 + tp < seq
    h = POOL_HALO
    ext_ref[0:h, :] = jnp.where(has_prev, up_ref[...], 0.0)
    ext_ref[h:h + tp, :] = u_ref[...]
    ext_ref[h + tp:h + tp + h, :] = jnp.where(has_next, un_ref[...], 0.0)

    gdim = wp_ref.shape[1]
    pos = pos0 + lax.broadcasted_iota(I32, (tp, 1), 0)
    for g, w in enumerate(POOL_WINDOWS):
        c0 = g * gdim
        acc = jnp.zeros((tp, gdim), F32)
        for j in range(-(w // 2), w - w // 2):
            acc = acc + ext_ref[h + j:h + j + tp, c0:c0 + gdim]
        lo = jnp.maximum(pos - w // 2, 0)
        hi = jnp.minimum(pos + (w - w // 2), seq)
        mean = acc / (hi - lo).astype(F32)
        pooled = (mean - u_ref[:, c0:c0 + gdim]).astype(BF16)
        y = jnp.dot(pooled, wp_ref[g], preferred_element_type=F32) * ps_ref[:, c0:c0 + gdim]
        o_ref[:, c0:c0 + gdim] = y.astype(o_ref.dtype)


def _pool_mixer(u2d, seq, w_pool_b, pool_scale):
    n, d_pool = u2d.shape
    tp = _pick(seq, (512, 256, 128))
    h = POOL_HALO
    assert max(POOL_WINDOWS) // 2 <= h
    nb = n // h
    per = tp // h
    return pl.pallas_call(
        functools.partial(_pool_kernel, tp=tp, seq=seq),
        out_shape=jax.ShapeDtypeStruct((n, d_pool), BF16),
        grid=(n // tp,),
        in_specs=[
            pl.BlockSpec((h, d_pool), lambda i: (jnp.maximum(i * per - 1, 0), 0)),
            pl.BlockSpec((tp, d_pool), lambda i: (i, 0)),
            pl.BlockSpec((h, d_pool), lambda i: (jnp.minimum((i + 1) * per, nb - 1), 0)),
            _resident(w_pool_b.shape, lambda i: (0, 0, 0)),
            _resident((1, d_pool), lambda i: (0, 0)),
        ],
        out_specs=pl.BlockSpec((tp, d_pool), lambda i: (i, 0)),
        scratch_shapes=[pltpu.VMEM((tp + 2 * h, d_pool), F32)],
        compiler_params=_cparams(("arbitrary",)),
        name="pool_mixer",
    )(u2d, u2d, u2d, w_pool_b, pool_scale.reshape(1, d_pool))


def _merge_kernel(h_ref, pm_ref, at_ref, wga_ref, wgb_ref, bga_ref, bgb_ref, wa_ref, wb_ref, o_ref):
    hb = h_ref[...]
    ga = jax.nn.sigmoid(jnp.dot(hb, wga_ref[...], preferred_element_type=F32) + bga_ref[...])
    gb = jax.nn.sigmoid(jnp.dot(hb, wgb_ref[...], preferred_element_type=F32) + bgb_ref[...])
    ya = jnp.dot(pm_ref[...], wa_ref[...], preferred_element_type=F32)
    yb = jnp.dot(at_ref[...], wb_ref[...], preferred_element_type=F32)
    o_ref[...] = (ga * ya + gb * yb).astype(o_ref.dtype)


def _gated_merge(h2d, pm2d, at2d, w_gate_b, b_gate, w_a_b, w_b_b):
    n, d = h2d.shape
    d_pool = pm2d.shape[1]
    d_q = at2d.shape[1]
    tm = _pick(n, (512, 256, 128))
    tn = _pick(d, (512, 256, 128))
    nj = d // tn
    bg = b_gate.reshape(1, 2 * d)
    return pl.pallas_call(
        _merge_kernel,
        out_shape=jax.ShapeDtypeStruct((n, d), BF16),
        grid=(n // tm, nj),
        in_specs=[
            pl.BlockSpec((tm, d), lambda i, j: (i, 0)),
            pl.BlockSpec((tm, d_pool), lambda i, j: (i, 0)),
            pl.BlockSpec((tm, d_q), lambda i, j: (i, 0)),
            pl.BlockSpec((d, tn), lambda i, j: (0, j)),
            pl.BlockSpec((d, tn), lambda i, j: (0, nj + j)),
            pl.BlockSpec((1, tn), lambda i, j: (0, j)),
            pl.BlockSpec((1, tn), lambda i, j: (0, nj + j)),
            pl.BlockSpec((d_pool, tn), lambda i, j: (0, j)),
            pl.BlockSpec((d_q, tn), lambda i, j: (0, j)),
        ],
        out_specs=pl.BlockSpec((tm, tn), lambda i, j: (i, j)),
        compiler_params=_cparams(("arbitrary", "arbitrary")),
        name="gated_merge",
    )(h2d, pm2d, at2d, w_gate_b, w_gate_b, bg, bg, w_a_b, w_b_b)


def _outproj_kernel(mg_ref, x_ref, mod_ref, wo_ref, g2_ref, wr_ref, br_ref, x1_ref, h2_ref, lg_ref):
    mix = jnp.dot(mg_ref[...], wo_ref[...], preferred_element_type=F32)
    gate1 = mod_ref[0, 2:3, :]
    shift2 = mod_ref[0, 3:4, :]
    scale2 = mod_ref[0, 4:5, :]
    x1 = x_ref[...] + gate1 * mix
    x1_ref[...] = x1
    ms = jnp.mean(x1 * x1, axis=-1, keepdims=True)
    h2 = (x1 * lax.rsqrt(ms + EPS) * g2_ref[...]) * (1.0 + scale2) + shift2
    h2_ref[...] = h2
    lg_ref[...] = jnp.dot(h2, wr_ref[...], preferred_element_type=F32,
                          precision=lax.Precision.HIGHEST) + br_ref[...]


def _out_projection(mg2d, x2d, mod3, mod_row, w_o_b, g2, w_router, b_router):
    n, d = x2d.shape
    ne = w_router.shape[1]
    tm = _pick(n, (256, 128))
    return pl.pallas_call(
        _outproj_kernel,
        out_shape=[
            jax.ShapeDtypeStruct((n, d), F32),
            jax.ShapeDtypeStruct((n, d), F32),
            jax.ShapeDtypeStruct((n, ne), F32),
        ],
        grid=(n // tm,),
        in_specs=[
            pl.BlockSpec((tm, d), lambda i: (i, 0)),
            pl.BlockSpec((tm, d), lambda i: (i, 0)),
            pl.BlockSpec((1, N_MOD, d), lambda i: (mod_row(i * tm), 0, 0)),
            _resident((d, d), lambda i: (0, 0)),
            _resident((1, d), lambda i: (0, 0)),
            _resident((d, ne), lambda i: (0, 0)),
            _resident((1, ne), lambda i: (0, 0)),
        ],
        out_specs=[
            pl.BlockSpec((tm, d), lambda i: (i, 0)),
            pl.BlockSpec((tm, d), lambda i: (i, 0)),
            pl.BlockSpec((tm, ne), lambda i: (i, 0)),
        ],
        compiler_params=_cparams(("arbitrary",)),
        name="out_projection",
    )(mg2d, x2d, mod3, w_o_b, g2.reshape(1, d), w_router, b_router.reshape(1, ne))


def _route_kernel(lg_ref, idx_ref, gw_ref, rk_ref, cnt_ref):
    step = pl.program_id(0)

    @pl.when(step == 0)
    def _():
        cnt_ref[...] = jnp.zeros_like(cnt_ref)

    lg = lg_ref[...]
    tr, ne = lg.shape
    lane = lax.broadcasted_iota(I32, (tr, ne), 1)
    work = lg
    vals, idxs = [], []
    member = jnp.zeros((tr, ne), F32)
    for _ in range(TOP_K):
        mx = jnp.max(work, axis=-1, keepdims=True)
        ix = jnp.min(jnp.where(work == mx, lane, ne), axis=-1, keepdims=True)
        sel = lane == ix
        vals.append(mx)
        idxs.append(ix)
        member = jnp.where(sel, 1.0, member)
        work = jnp.where(sel, -jnp.inf, work)

    ex = [jnp.exp(v - vals[0]) for v in vals]
    den = ex[0]
    for e in ex[1:]:
        den = den + e

    row = lax.broadcasted_iota(I32, (tr, tr), 0)
    col = lax.broadcasted_iota(I32, (tr, tr), 1)
    lower = jnp.where(col < row, 1.0, 0.0).astype(BF16)
    before = jnp.dot(lower, member.astype(BF16), preferred_element_type=F32) + cnt_ref[...]

    k_lane = lax.broadcasted_iota(I32, (tr, TOP_K), 1)
    idx_out = jnp.zeros((tr, TOP_K), I32)
    gw_out = jnp.zeros((tr, TOP_K), F32)
    rk_out = jnp.zeros((tr, TOP_K), I32)
    for k in range(TOP_K):
        rk = jnp.sum(jnp.where(lane == idxs[k], before, 0.0), axis=-1, keepdims=True).astype(I32)
        idx_out = jnp.where(k_lane == k, idxs[k], idx_out)
        gw_out = jnp.where(k_lane == k, ex[k] / den, gw_out)
        rk_out = jnp.where(k_lane == k, rk, rk_out)
    idx_ref[...] = idx_out
    gw_ref[...] = gw_out
    rk_ref[...] = rk_out
    cnt_ref[...] += jnp.sum(member, axis=0, keepdims=True)


def _routing(logits):
    n, ne = logits.shape
    tr = _pick(n, (512, 256, 128))
    return pl.pallas_call(
        _route_kernel,
        out_shape=[
            jax.ShapeDtypeStruct((n, TOP_K), I32),
            jax.ShapeDtypeStruct((n, TOP_K), F32),
            jax.ShapeDtypeStruct((n, TOP_K), I32),
            jax.ShapeDtypeStruct((1, ne), F32),
        ],
        grid=(n // tr,),
        in_specs=[pl.BlockSpec((tr, ne), lambda i: (i, 0))],
        out_specs=[
            pl.BlockSpec((tr, TOP_K), lambda i: (i, 0)),
            pl.BlockSpec((tr, TOP_K), lambda i: (i, 0)),
            pl.BlockSpec((tr, TOP_K), lambda i: (i, 0)),
            pl.BlockSpec((1, ne), lambda i: (0, 0)),
        ],
        compiler_params=_cparams(("arbitrary",)),
        name="routing",
    )(logits)


def _dispatch_kernel(dest_ref, h_hbm, xs_in_hbm, xs_hbm, sem, *, td):
    del xs_in_hbm
    base = pl.program_id(0) * td

    def row_copy(r, k):
        d = dest_ref[0, 0, r * TOP_K + k]
        return pltpu.make_async_copy(h_hbm.at[pl.ds(base + r, 1)], xs_hbm.at[pl.ds(d, 1)], sem)

    def start(r, carry):
        for k in range(TOP_K):
            row_copy(r, k).start()
        return carry

    def wait(r, carry):
        for k in range(TOP_K):
            row_copy(r, k).wait()
        return carry

    lax.fori_loop(0, td, start, 0)
    lax.fori_loop(0, td, wait, 0)


def _dispatch(h2d, dest, xs0):
    n, d = h2d.shape
    n_slots = xs0.shape[0]
    td = _pick(n, (256, 128))
    dest3 = dest.reshape(n // td, 1, td * TOP_K)
    return pl.pallas_call(
        functools.partial(_dispatch_kernel, td=td),
        out_shape=jax.ShapeDtypeStruct((n_slots, d), h2d.dtype),
        grid=(n // td,),
        in_specs=[
            pl.BlockSpec((1, 1, td * TOP_K), lambda i: (i, 0, 0), memory_space=pltpu.SMEM),
            pl.BlockSpec(memory_space=pl.ANY),
            pl.BlockSpec(memory_space=pl.ANY),
        ],
        out_specs=pl.BlockSpec(memory_space=pl.ANY),
        scratch_shapes=[pltpu.SemaphoreType.DMA(())],
        input_output_aliases={2: 0},
        compiler_params=_cparams(("arbitrary",)),
        name="dispatch",
    )(dest3, h2d, xs0)


def _expert_kernel(te_ref, nu_ref, x_ref, wg_ref, wl_ref, bg_ref, bl_ref, wd_ref, bd_ref, y_ref,
                   xb_ref, acc_ref, *, nf):
    del te_ref
    i = pl.program_id(0)
    f = pl.program_id(1)
    used = i < nu_ref[0]

    @pl.when(used)
    def _():
        @pl.when(f == 0)
        def _():
            xb_ref[...] = x_ref[...].astype(BF16)
            acc_ref[...] = jnp.zeros_like(acc_ref)

        xb = xb_ref[...]
        g = jnp.dot(xb, wg_ref[0], preferred_element_type=F32) + bg_ref[0]
        lin = jnp.dot(xb, wl_ref[0], preferred_element_type=F32) + bl_ref[0]
        g = jnp.minimum(g, SWIGLU_LIMIT)
        lin = jnp.clip(lin, -SWIGLU_LIMIT, SWIGLU_LIMIT)
        act = g * jax.nn.sigmoid(SWIGLU_ALPHA * g) * (lin + 1.0)
        acc_ref[...] += jnp.dot(act.astype(BF16), wd_ref[0], preferred_element_type=F32)

        @pl.when(f == nf - 1)
        def _():
            y_ref[...] = acc_ref[...] + bd_ref[0]

    @pl.when(jnp.logical_and(jnp.logical_not(used), f == nf - 1))
    def _():
        y_ref[...] = jnp.zeros_like(y_ref)


def _expert_mlp(xs, tile_expert, n_used, w_gu_b, b_gu, w_down_b, b_down, te):
    n_slots, d = xs.shape
    ne, _, d_ff2 = w_gu_b.shape
    d_ff = d_ff2 // 2
    tf = _pick(d_ff, (512, 256, 128))
    nf = d_ff // tf
    n_tiles = n_slots // te

    def tile(i, nu):
        return jnp.minimum(i, nu[0] - 1)

    def ff(i, f, nu):
        return jnp.where(i < nu[0], f, nf - 1)

    grid_spec = pltpu.PrefetchScalarGridSpec(
        num_scalar_prefetch=2,
        grid=(n_tiles, nf),
        in_specs=[
            pl.BlockSpec((te, d), lambda i, f, tx, nu: (tile(i, nu), 0)),
            pl.BlockSpec((1, d, tf), lambda i, f, tx, nu: (tx[tile(i, nu)], 0, ff(i, f, nu))),
            pl.BlockSpec((1, d, tf), lambda i, f, tx, nu: (tx[tile(i, nu)], 0, nf + ff(i, f, nu))),
            pl.BlockSpec((1, 1, tf), lambda i, f, tx, nu: (tx[tile(i, nu)], 0, ff(i, f, nu))),
            pl.BlockSpec((1, 1, tf), lambda i, f, tx, nu: (tx[tile(i, nu)], 0, nf + ff(i, f, nu))),
            pl.BlockSpec((1, tf, d), lambda i, f, tx, nu: (tx[tile(i, nu)], ff(i, f, nu), 0)),
            pl.BlockSpec((1, 1, d), lambda i, f, tx, nu: (tx[tile(i, nu)], 0, 0)),
        ],
        out_specs=pl.BlockSpec((te, d), lambda i, f, tx, nu: (i, 0)),
        scratch_shapes=[pltpu.VMEM((te, d), BF16), pltpu.VMEM((te, d), F32)],
    )
    return pl.pallas_call(
        functools.partial(_expert_kernel, nf=nf),
        out_shape=jax.ShapeDtypeStruct((n_slots, d), F32),
        grid_spec=grid_spec,
        compiler_params=_cparams(("arbitrary", "arbitrary")),
        name="expert_mlp",
    )(tile_expert, n_used, xs, w_gu_b, w_gu_b, b_gu.reshape(ne, 1, d_ff2), b_gu.reshape(ne, 1, d_ff2),
      w_down_b, b_down.reshape(ne, 1, d))


def _combine_kernel(dest_ref, x1_ref, gw_ref, mod_ref, gf_ref, y_hbm, o_ref, ybuf, sem, *, tc):
    def row_copy(r, k):
        d = dest_ref[0, 0, r * TOP_K + k]
        return pltpu.make_async_copy(y_hbm.at[pl.ds(d, 1)], ybuf.at[k, pl.ds(r, 1)], sem)

    def start(r, carry):
        for k in range(TOP_K):
            row_copy(r, k).start()
        return carry

    def wait(r, carry):
        for k in range(TOP_K):
            row_copy(r, k).wait()
        return carry

    lax.fori_loop(0, tc, start, 0)
    lax.fori_loop(0, tc, wait, 0)

    gw = gw_ref[...]
    moe = gw[:, 0:1] * ybuf[0]
    for k in range(1, TOP_K):
        moe = moe + gw[:, k:k + 1] * ybuf[k]
    gate2 = mod_ref[0, 5:6, :]
    x2 = x1_ref[...] + gate2 * moe
    ms = jnp.mean(x2 * x2, axis=-1, keepdims=True)
    o_ref[...] = x2 * lax.rsqrt(ms + EPS) * gf_ref[...]


def _combine(x1, gate_w, dest, y, mod3, mod_row, final_g):
    n, d = x1.shape
    tc = _pick(n, (128,))
    dest3 = dest.reshape(n // tc, 1, tc * TOP_K)
    return pl.pallas_call(
        functools.partial(_combine_kernel, tc=tc),
        out_shape=jax.ShapeDtypeStruct((n, d), F32),
        grid=(n // tc,),
        in_specs=[
            pl.BlockSpec((1, 1, tc * TOP_K), lambda i: (i, 0, 0), memory_space=pltpu.SMEM),
            pl.BlockSpec((tc, d), lambda i: (i, 0)),
            pl.BlockSpec((tc, TOP_K), lambda i: (i, 0)),
            pl.BlockSpec((1, N_MOD, d), lambda i: (mod_row(i * tc), 0, 0)),
            _resident((1, d), lambda i: (0, 0)),
            pl.BlockSpec(memory_space=pl.ANY),
        ],
        out_specs=pl.BlockSpec((tc, d), lambda i: (i, 0)),
        scratch_shapes=[pltpu.VMEM((TOP_K, tc, d), F32), pltpu.SemaphoreType.DMA(())],
        compiler_params=_cparams(("arbitrary",)),
        name="combine",
    )(dest3, x1, gate_w, mod3, final_g.reshape(1, d), y)


def _rope_tables(seq):
    rows = seq // GRID_W
    row = jnp.repeat(jnp.arange(rows), GRID_W).astype(F32)
    col = jnp.tile(jnp.arange(GRID_W), rows).astype(F32)
    freqs = ROPE_THETA ** (-jnp.arange(ROPE_PAIRS_PER_AXIS, dtype=F32) / ROPE_PAIRS_PER_AXIS)
    ang = jnp.concatenate([row[:, None] * freqs, col[:, None] * freqs], axis=-1)
    cos, sin = jnp.cos(ang), jnp.sin(ang)
    return jnp.concatenate([cos, cos], axis=-1), jnp.concatenate([-sin, sin], axis=-1)


def _moe(groups, wts, final_g, mod3):
    te = EXPERT_TILE_ROWS
    sizes = [g[0].shape[0] for g in groups]
    n = sum(sizes)
    logits = jnp.concatenate([g[2] for g in groups], axis=0)
    ne = logits.shape[1]
    top_i, gate_w, rank, counts = _routing(logits)

    counts = counts.reshape(ne).astype(I32)
    padded = (counts + te - 1) // te * te
    pad_end = jnp.cumsum(padded)
    pad_start = pad_end - padded
    n_tiles = (n * TOP_K) // te + ne
    n_slots = n_tiles * te
    tile_expert = jnp.minimum(
        jnp.searchsorted(pad_end, jnp.arange(n_tiles, dtype=I32) * te, side="right"), ne - 1).astype(I32)
    n_used = (pad_end[-1:] // te).astype(I32)
    dest = pad_start[top_i] + rank

    offs = [sum(sizes[:i]) for i in range(len(sizes))]
    xs = jnp.zeros((n_slots, groups[0][1].shape[1]), F32)
    for (_, h2, _, _), o, sz in zip(groups, offs, sizes):
        xs = _dispatch(h2, dest[o:o + sz], xs)
    y = _expert_mlp(xs, tile_expert, n_used, wts["w_gu"], wts["b_gu"], wts["w_down"], wts["b_down"], te)
    return [_combine(x1, gate_w[o:o + sz], dest[o:o + sz], y, mod3, mod_row, final_g)
            for (x1, _, _, mod_row), o, sz in zip(groups, offs, sizes)]


def _mixer(x2d, batch, seq, mod3, mod_row, rope, kv_ctx, wts, emit_kv_f32):
    outs = _in_projection(x2d, seq, mod3, mod_row, wts["norm1_g"], wts["w_in"], wts["q_norm_g"],
                          wts["k_norm_g"], rope, emit_kv_f32)
    h, u, q, k, v = outs[:5]
    d_q = q.shape[1]
    d_kv = k.shape[1]
    q3 = q.reshape(batch, seq, d_q)
    k3 = k.reshape(batch, seq, d_kv)
    v3 = v.reshape(batch, seq, d_kv)
    if kv_ctx is not None:
        k3 = jnp.concatenate([kv_ctx[0], k3], axis=1)
        v3 = jnp.concatenate([kv_ctx[1], v3], axis=1)
    attn = _attention(q3, k3, v3).reshape(batch * seq, d_q)
    pm = _pool_mixer(u, seq, wts["w_pool"], wts["pool_scale"])
    merged = _gated_merge(h, pm, attn, wts["w_gate"], wts["b_gate"], wts["w_a_out"], wts["w_b_out"])
    x1, h2, logits = _out_projection(merged, x2d, mod3, mod_row, wts["w_o"], wts["norm2_g"],
                                     wts["w_router"], wts["b_router"])
    return (x1, h2, logits, mod_row), outs[5:]


def kernel(x_prompt, x_sample, cache_k, cache_v, c, c_ctx, w_mod, b_mod, norm1_g, w_in, q_norm_g, k_norm_g, w_pool, pool_scale, w_a_out, w_b_out, w_gate, b_gate, w_o, norm2_g, w_router, b_router, w_gu, b_gu, w_down, b_down, final_g):
    depth = w_mod.shape[0]
    assert depth == 1, "the final norm is fused into the last combine; one trunk layer is supported"
    bp, sp, d = x_prompt.shape
    bs, ss, _ = x_sample.shape
    l = 0

    n_rows = -(-(bs + 1) // SUBLANES) * SUBLANES
    cvec = jnp.zeros((n_rows, d), F32).at[:bs].set(c).at[bs].set(c_ctx)
    mod3 = _modulation(cvec, w_mod[l], b_mod[l]).reshape(n_rows, N_MOD, d)

    wts = dict(
        norm1_g=norm1_g[l], w_in=w_in[l].astype(BF16), q_norm_g=q_norm_g[l], k_norm_g=k_norm_g[l],
        w_pool=w_pool[l].astype(BF16), pool_scale=pool_scale[l], w_a_out=w_a_out[l].astype(BF16),
        w_b_out=w_b_out[l].astype(BF16), w_gate=w_gate[l].astype(BF16), b_gate=b_gate[l],
        w_o=w_o[l].astype(BF16), norm2_g=norm2_g[l], w_router=w_router[l], b_router=b_router[l],
        w_gu=w_gu[l].astype(BF16), b_gu=b_gu[l], w_down=w_down[l].astype(BF16), b_down=b_down[l],
    )

    gp, (kf, vf) = _mixer(x_prompt.reshape(bp * sp, d), bp, sp, mod3, lambda t: bs, None, None, wts, True)
    kv_ctx = (cache_k[:, l].reshape(bs, -1, N_KV_HEADS * HEAD_DIM).astype(BF16),
              cache_v[:, l].reshape(bs, -1, N_KV_HEADS * HEAD_DIM).astype(BF16))
    gs, _ = _mixer(x_sample.reshape(bs * ss, d), bs, ss, mod3, lambda t: t // ss, _rope_tables(ss),
                   kv_ctx, wts, False)
    yp, ys = _moe([gp, gs], wts, final_g, mod3)

    y_prompt = yp.reshape(bp, sp, d)
    y_sample = ys.reshape(bs, ss, d)
    new_cache_k = kf.reshape(bp, 1, sp, N_KV_HEADS, HEAD_DIM)
    new_cache_v = vf.reshape(bp, 1, sp, N_KV_HEADS, HEAD_DIM)
    return (y_prompt, y_sample, new_cache_k, new_cache_v)
```

```python
import functools

import jax
import jax.numpy as jnp
from jax import lax
from jax.experimental import pallas as pl
from jax.experimental.pallas import tpu as pltpu

F32 = jnp.float32
BF16 = jnp.bfloat16
I32 = jnp.int32

GRID_W = 64
N_HEADS = 16
N_KV_HEADS = 4
HEAD_DIM = 128
Q_GROUP = N_HEADS // N_KV_HEADS
ROPE_PAIRS_PER_AXIS = HEAD_DIM // 4
ROPE_THETA = 10000.0
POOL_WINDOWS = (2, 4, 8, 16)
TOP_K = 4
SWIGLU_LIMIT = 7.0
SWIGLU_ALPHA = 1.702
N_MOD = 6
EPS = 1e-6

LANES = 128
SUBLANES = 8
VMEM_LIMIT_BYTES = 56 * 1024 * 1024

EXPERT_TILE_ROWS = 512
POOL_HALO = SUBLANES


def _cparams(semantics):
    return pltpu.CompilerParams(dimension_semantics=semantics, vmem_limit_bytes=VMEM_LIMIT_BYTES)


def _pick(n, candidates):
    for c in candidates:
        if n % c == 0:
            return c
    raise ValueError(f"no tile in {candidates} divides {n}")


def _resident(shape, index_map):
    return pl.BlockSpec(shape, index_map, pipeline_mode=pl.Buffered(1))


def _mod_kernel(c_ref, w_ref, b_ref, o_ref):
    c = c_ref[...]
    s = (c * jax.nn.sigmoid(c)).astype(BF16)
    o_ref[...] = jnp.dot(s, w_ref[...].astype(BF16), preferred_element_type=F32) + b_ref[...]


def _modulation(cvec, w_mod, b_mod):
    r, d = cvec.shape
    n6 = w_mod.shape[1]
    tn = _pick(n6, (1024, 512, 256, 128))
    return pl.pallas_call(
        _mod_kernel,
        out_shape=jax.ShapeDtypeStruct((r, n6), F32),
        grid=(n6 // tn,),
        in_specs=[
            pl.BlockSpec((r, d), lambda j: (0, 0)),
            pl.BlockSpec((d, tn), lambda j: (0, j)),
            pl.BlockSpec((1, tn), lambda j: (0, j)),
        ],
        out_specs=pl.BlockSpec((r, tn), lambda j: (0, j)),
        compiler_params=_cparams(("arbitrary",)),
        name="modulation",
    )(cvec, w_mod, b_mod.reshape(1, n6))


def _inproj_kernel(*refs, use_rope, emit_kv_f32, d_pool):
    it = iter(refs)
    x_ref, mod_ref, g1_ref, w_ref, qg_ref, kg_ref = (next(it) for _ in range(6))
    cos_ref = sin_ref = None
    if use_rope:
        cos_ref, sin_ref = next(it), next(it)
    h_ref, u_ref, q_ref, k_ref, v_ref = (next(it) for _ in range(5))
    kf_ref = vf_ref = None
    if emit_kv_f32:
        kf_ref, vf_ref = next(it), next(it)

    x = x_ref[...]
    ms = jnp.mean(x * x, axis=-1, keepdims=True)
    xn = x * lax.rsqrt(ms + EPS) * g1_ref[...]
    shift = mod_ref[0, 0:1, :]
    scale = mod_ref[0, 1:2, :]
    hb = (xn * (1.0 + scale) + shift).astype(BF16)
    h_ref[...] = hb

    d_q = N_HEADS * HEAD_DIM
    d_kv = N_KV_HEADS * HEAD_DIM
    chunk = Q_GROUP * HEAD_DIM

    for c0 in range(0, d_pool, chunk):
        u_ref[:, c0:c0 + chunk] = jnp.dot(hb, w_ref[:, c0:c0 + chunk], preferred_element_type=F32)

    if use_rope:
        cos = cos_ref[...]
        sin = sin_ref[...]

    def head(xh, g):
        hm = jnp.mean(xh * xh, axis=-1, keepdims=True)
        y = xh * lax.rsqrt(hm + EPS) * g
        if use_rope:
            y = y * cos + pltpu.roll(y, HEAD_DIM // 2, axis=1) * sin
        return y

    qg = qg_ref[...]
    kg = kg_ref[...]
    for c0 in range(0, d_q, chunk):
        pr = jnp.dot(hb, w_ref[:, d_pool + c0:d_pool + c0 + chunk], preferred_element_type=F32)
        for j in range(chunk // HEAD_DIM):
            yh = head(pr[:, j * HEAD_DIM:(j + 1) * HEAD_DIM], qg)
            q_ref[:, c0 + j * HEAD_DIM:c0 + (j + 1) * HEAD_DIM] = yh.astype(BF16)

    pk = jnp.dot(hb, w_ref[:, d_pool + d_q:d_pool + d_q + d_kv], preferred_element_type=F32)
    for j in range(N_KV_HEADS):
        yh = head(pk[:, j * HEAD_DIM:(j + 1) * HEAD_DIM], kg)
        k_ref[:, j * HEAD_DIM:(j + 1) * HEAD_DIM] = yh.astype(BF16)
        if emit_kv_f32:
            kf_ref[:, j * HEAD_DIM:(j + 1) * HEAD_DIM] = yh

    pv = jnp.dot(hb, w_ref[:, d_pool + d_q + d_kv:], preferred_element_type=F32)
    v_ref[...] = pv.astype(BF16)
    if emit_kv_f32:
        vf_ref[...] = pv


def _in_projection(x2d, seq, mod3, mod_row, g1, w_in_b, qg, kg, rope, emit_kv_f32):
    n, d = x2d.shape
    d_in = w_in_b.shape[1]
    d_q = N_HEADS * HEAD_DIM
    d_kv = N_KV_HEADS * HEAD_DIM
    d_pool = d_in - d_q - 2 * d_kv
    ts = _pick(seq, (512, 256, 128))
    tiles_per_seq = seq // ts
    use_rope = rope is not None

    in_specs = [
        pl.BlockSpec((ts, d), lambda i: (i, 0)),
        pl.BlockSpec((1, N_MOD, d), lambda i: (mod_row(i * ts), 0, 0)),
        _resident((1, d), lambda i: (0, 0)),
        _resident((d, d_in), lambda i: (0, 0)),
        _resident((1, HEAD_DIM), lambda i: (0, 0)),
        _resident((1, HEAD_DIM), lambda i: (0, 0)),
    ]
    args = [x2d, mod3, g1.reshape(1, d), w_in_b, qg.reshape(1, HEAD_DIM), kg.reshape(1, HEAD_DIM)]
    if use_rope:
        in_specs += [pl.BlockSpec((ts, HEAD_DIM), lambda i: (i % tiles_per_seq, 0))] * 2
        args += list(rope)

    out_shape = [
        jax.ShapeDtypeStruct((n, d), BF16),
        jax.ShapeDtypeStruct((n, d_pool), F32),
        jax.ShapeDtypeStruct((n, d_q), BF16),
        jax.ShapeDtypeStruct((n, d_kv), BF16),
        jax.ShapeDtypeStruct((n, d_kv), BF16),
    ]
    out_specs = [
        pl.BlockSpec((ts, d), lambda i: (i, 0)),
        pl.BlockSpec((ts, d_pool), lambda i: (i, 0)),
        pl.BlockSpec((ts, d_q), lambda i: (i, 0)),
        pl.BlockSpec((ts, d_kv), lambda i: (i, 0)),
        pl.BlockSpec((ts, d_kv), lambda i: (i, 0)),
    ]
    if emit_kv_f32:
        out_shape += [jax.ShapeDtypeStruct((n, d_kv), F32)] * 2
        out_specs += [pl.BlockSpec((ts, d_kv), lambda i: (i, 0))] * 2

    return pl.pallas_call(
        functools.partial(_inproj_kernel, use_rope=use_rope, emit_kv_f32=emit_kv_f32, d_pool=d_pool),
        out_shape=out_shape,
        grid=(n // ts,),
        in_specs=in_specs,
        out_specs=out_specs,
        compiler_params=_cparams(("arbitrary",)),
        name="in_projection",
    )(*args)


def _attn_kernel(q_ref, k_ref, v_ref, o_ref, *, tk, n_chunks, scale):
    q = q_ref[0]
    tq = q.shape[0]
    qs = jnp.concatenate([q[:, j * HEAD_DIM:(j + 1) * HEAD_DIM] for j in range(Q_GROUP)], axis=0)
    rows = qs.shape[0]
    c2 = scale * 1.4426950408889634
    m = jnp.full((rows, 1), -jnp.inf, F32)
    l = jnp.zeros((rows, 1), F32)
    acc = jnp.zeros((rows, HEAD_DIM), F32)
    for c in range(n_chunks):
        kc = k_ref[0, c * tk:(c + 1) * tk, :]
        vc = v_ref[0, c * tk:(c + 1) * tk, :]
        s = lax.dot_general(qs, kc, (((1,), (1,)), ((), ())), preferred_element_type=F32) * c2
        m_new = jnp.maximum(m, jnp.max(s, axis=1, keepdims=True))
        alpha = jnp.exp2(m - m_new)
        p = jnp.exp2(s - m_new)
        l = alpha * l + jnp.sum(p, axis=1, keepdims=True)
        acc = alpha * acc + jnp.dot(p.astype(BF16), vc, preferred_element_type=F32)
        m = m_new
    out = acc / l
    o_ref[0] = jnp.concatenate([out[j * tq:(j + 1) * tq] for j in range(Q_GROUP)], axis=1).astype(o_ref.dtype)


def _attention(q3, k3, v3):
    b, s, d_q = q3.shape
    kv_len = k3.shape[1]
    tq = _pick(s, (256, 128))
    tk = _pick(kv_len, (256, 128))
    gw = Q_GROUP * HEAD_DIM
    return pl.pallas_call(
        functools.partial(_attn_kernel, tk=tk, n_chunks=kv_len // tk, scale=HEAD_DIM ** -0.5),
        out_shape=jax.ShapeDtypeStruct((b, s, d_q), BF16),
        grid=(b, N_KV_HEADS, s // tq),
        in_specs=[
            pl.BlockSpec((1, tq, gw), lambda bi, g, qi: (bi, qi, g)),
            pl.BlockSpec((1, kv_len, HEAD_DIM), lambda bi, g, qi: (bi, 0, g)),
            pl.BlockSpec((1, kv_len, HEAD_DIM), lambda bi, g, qi: (bi, 0, g)),
        ],
        out_specs=pl.BlockSpec((1, tq, gw), lambda bi, g, qi: (bi, qi, g)),
        compiler_params=_cparams(("arbitrary", "arbitrary", "arbitrary")),
        name="attention",
    )(q3, k3, v3)


def _pool_kernel(up_ref, u_ref, un_ref, wp_ref, ps_ref, o_ref, ext_ref, *, tp, seq):
    i = pl.program_id(0)
    pos0 = (i * tp) % seq
    has_prev = pos0 > 0
    has_next = pos0 + tp < seq
    h = POOL_HALO
    ext_ref[0:h, :] = jnp.where(has_prev, up_ref[...], 0.0)
    ext_ref[h:h + tp, :] = u_ref[...]
    ext_ref[h + tp:h + tp + h, :] = jnp.where(has_next, un_ref[...], 0.0)

    gdim = wp_ref.shape[1]
    pos = pos0 + lax.broadcasted_iota(I32, (tp, 1), 0)
    for g, w in enumerate(POOL_WINDOWS):
        c0 = g * gdim
        acc = jnp.zeros((tp, gdim), F32)
        for j in range(-(w // 2), w - w // 2):
            acc = acc + ext_ref[h + j:h + j + tp, c0:c0 + gdim]
        lo = jnp.maximum(pos - w // 2, 0)
        hi = jnp.minimum(pos + (w - w // 2), seq)
        mean = acc / (hi - lo).astype(F32)
        pooled = (mean - u_ref[:, c0:c0 + gdim]).astype(BF16)
        y = jnp.dot(pooled, wp_ref[g], preferred_element_type=F32) * ps_ref[:, c0:c0 + gdim]
        o_ref[:, c0:c0 + gdim] = y.astype(o_ref.dtype)


def _pool_mixer(u2d, seq, w_pool_b, pool_scale):
    n, d_pool = u2d.shape
    tp = _pick(seq, (512, 256, 128))
    h = POOL_HALO
    assert max(POOL_WINDOWS) // 2 <= h
    nb = n // h
    per = tp // h
    return pl.pallas_call(
        functools.partial(_pool_kernel, tp=tp, seq=seq),
        out_shape=jax.ShapeDtypeStruct((n, d_pool), BF16),
        grid=(n // tp,),
        in_specs=[
            pl.BlockSpec((h, d_pool), lambda i: (jnp.maximum(i * per - 1, 0), 0)),
            pl.BlockSpec((tp, d_pool), lambda i: (i, 0)),
            pl.BlockSpec((h, d_pool), lambda i: (jnp.minimum((i + 1) * per, nb - 1), 0)),
            _resident(w_pool_b.shape, lambda i: (0, 0, 0)),
            _resident((1, d_pool), lambda i: (0, 0)),
        ],
        out_specs=pl.BlockSpec((tp, d_pool), lambda i: (i, 0)),
        scratch_shapes=[pltpu.VMEM((tp + 2 * h, d_pool), F32)],
        compiler_params=_cparams(("arbitrary",)),
        name="pool_mixer",
    )(u2d, u2d, u2d, w_pool_b, pool_scale.reshape(1, d_pool))


def _merge_kernel(h_ref, pm_ref, at_ref, wga_ref, wgb_ref, bga_ref, bgb_ref, wa_ref, wb_ref, o_ref):
    hb = h_ref[...]
    ga = jax.nn.sigmoid(jnp.dot(hb, wga_ref[...], preferred_element_type=F32) + bga_ref[...])
    gb = jax.nn.sigmoid(jnp.dot(hb, wgb_ref[...], preferred_element_type=F32) + bgb_ref[...])
    ya = jnp.dot(pm_ref[...], wa_ref[...], preferred_element_type=F32)
    yb = jnp.dot(at_ref[...], wb_ref[...], preferred_element_type=F32)
    o_ref[...] = (ga * ya + gb * yb).astype(o_ref.dtype)


def _gated_merge(h2d, pm2d, at2d, w_gate_b, b_gate, w_a_b, w_b_b):
    n, d = h2d.shape
    d_pool = pm2d.shape[1]
    d_q = at2d.shape[1]
    tm = _pick(n, (512, 256, 128))
    tn = _pick(d, (512, 256, 128))
    nj = d // tn
    bg = b_gate.reshape(1, 2 * d)
    return pl.pallas_call(
        _merge_kernel,
        out_shape=jax.ShapeDtypeStruct((n, d), BF16),
        grid=(n // tm, nj),
        in_specs=[
            pl.BlockSpec((tm, d), lambda i, j: (i, 0)),
            pl.BlockSpec((tm, d_pool), lambda i, j: (i, 0)),
            pl.BlockSpec((tm, d_q), lambda i, j: (i, 0)),
            pl.BlockSpec((d, tn), lambda i, j: (0, j)),
            pl.BlockSpec((d, tn), lambda i, j: (0, nj + j)),
            pl.BlockSpec((1, tn), lambda i, j: (0, j)),
            pl.BlockSpec((1, tn), lambda i, j: (0, nj + j)),
            pl.BlockSpec((d_pool, tn), lambda i, j: (0, j)),
            pl.BlockSpec((d_q, tn), lambda i, j: (0, j)),
        ],
        out_specs=pl.BlockSpec((tm, tn), lambda i, j: (i, j)),
        compiler_params=_cparams(("arbitrary", "arbitrary")),
        name="gated_merge",
    )(h2d, pm2d, at2d, w_gate_b, w_gate_b, bg, bg, w_a_b, w_b_b)


def _outproj_kernel(mg_ref, x_ref, mod_ref, wo_ref, g2_ref, wr_ref, br_ref, x1_ref, h2_ref, lg_ref):
    mix = jnp.dot(mg_ref[...], wo_ref[...], preferred_element_type=F32)
    gate1 = mod_ref[0, 2:3, :]
    shift2 = mod_ref[0, 3:4, :]
    scale2 = mod_ref[0, 4:5, :]
    x1 = x_ref[...] + gate1 * mix
    x1_ref[...] = x1
    ms = jnp.mean(x1 * x1, axis=-1, keepdims=True)
    h2 = (x1 * lax.rsqrt(ms + EPS) * g2_ref[...]) * (1.0 + scale2) + shift2
    h2_ref[...] = h2
    lg_ref[...] = jnp.dot(h2.astype(BF16), wr_ref[...], preferred_element_type=F32) + br_ref[...]


def _out_projection(mg2d, x2d, mod3, mod_row, w_o_b, g2, w_router, b_router):
    n, d = x2d.shape
    ne = w_router.shape[1]
    tm = _pick(n, (256, 128))
    return pl.pallas_call(
        _outproj_kernel,
        out_shape=[
            jax.ShapeDtypeStruct((n, d), F32),
            jax.ShapeDtypeStruct((n, d), F32),
            jax.ShapeDtypeStruct((n, ne), F32),
        ],
        grid=(n // tm,),
        in_specs=[
            pl.BlockSpec((tm, d), lambda i: (i, 0)),
            pl.BlockSpec((tm, d), lambda i: (i, 0)),
            pl.BlockSpec((1, N_MOD, d), lambda i: (mod_row(i * tm), 0, 0)),
            _resident((d, d), lambda i: (0, 0)),
            _resident((1, d), lambda i: (0, 0)),
            _resident((d, ne), lambda i: (0, 0)),
            _resident((1, ne), lambda i: (0, 0)),
        ],
        out_specs=[
            pl.BlockSpec((tm, d), lambda i: (i, 0)),
            pl.BlockSpec((tm, d), lambda i: (i, 0)),
            pl.BlockSpec((tm, ne), lambda i: (i, 0)),
        ],
        compiler_params=_cparams(("arbitrary",)),
        name="out_projection",
    )(mg2d, x2d, mod3, w_o_b, g2.reshape(1, d), w_router, b_router.reshape(1, ne))


def _route_kernel(lg_ref, idx_ref, gw_ref, rk_ref, cnt_ref):
    step = pl.program_id(0)

    @pl.when(step == 0)
    def _():
        cnt_ref[...] = jnp.zeros_like(cnt_ref)

    lg = lg_ref[...]
    tr, ne = lg.shape
    lane = lax.broadcasted_iota(I32, (tr, ne), 1)
    work = lg
    vals, idxs = [], []
    member = jnp.zeros((tr, ne), F32)
    for _ in range(TOP_K):
        mx = jnp.max(work, axis=-1, keepdims=True)
        ix = jnp.min(jnp.where(work == mx, lane, ne), axis=-1, keepdims=True)
        sel = lane == ix
        vals.append(mx)
        idxs.append(ix)
        member = jnp.where(sel, 1.0, member)
        work = jnp.where(sel, -jnp.inf, work)

    ex = [jnp.exp(v - vals[0]) for v in vals]
    den = ex[0]
    for e in ex[1:]:
        den = den + e

    row = lax.broadcasted_iota(I32, (tr, tr), 0)
    col = lax.broadcasted_iota(I32, (tr, tr), 1)
    lower = jnp.where(col < row, 1.0, 0.0).astype(BF16)
    before = jnp.dot(lower, member.astype(BF16), preferred_element_type=F32) + cnt_ref[...]

    k_lane = lax.broadcasted_iota(I32, (tr, TOP_K), 1)
    idx_out = jnp.zeros((tr, TOP_K), I32)
    gw_out = jnp.zeros((tr, TOP_K), F32)
    rk_out = jnp.zeros((tr, TOP_K), I32)
    for k in range(TOP_K):
        rk = jnp.sum(jnp.where(lane == idxs[k], before, 0.0), axis=-1, keepdims=True).astype(I32)
        idx_out = jnp.where(k_lane == k, idxs[k], idx_out)
        gw_out = jnp.where(k_lane == k, ex[k] / den, gw_out)
        rk_out = jnp.where(k_lane == k, rk, rk_out)
    idx_ref[...] = idx_out
    gw_ref[...] = gw_out
    rk_ref[...] = rk_out
    cnt_ref[...] += jnp.sum(member, axis=0, keepdims=True)


def _routing(logits):
    n, ne = logits.shape
    tr = _pick(n, (512, 256, 128))
    return pl.pallas_call(
        _route_kernel,
        out_shape=[
            jax.ShapeDtypeStruct((n, TOP_K), I32),
            jax.ShapeDtypeStruct((n, TOP_K), F32),
            jax.ShapeDtypeStruct((n, TOP_K), I32),
            jax.ShapeDtypeStruct((1, ne), F32),
        ],
        grid=(n // tr,),
        in_specs=[pl.BlockSpec((tr, ne), lambda i: (i, 0))],
        out_specs=[
            pl.BlockSpec((tr, TOP_K), lambda i: (i, 0)),
            pl.BlockSpec((tr, TOP_K), lambda i: (i, 0)),
            pl.BlockSpec((tr, TOP_K), lambda i: (i, 0)),
            pl.BlockSpec((1, ne), lambda i: (0, 0)),
        ],
        compiler_params=_cparams(("arbitrary",)),
        name="routing",
    )(logits)


def _dispatch_kernel(dest_ref, h_ref, xs_in_hbm, xs_hbm, sem, *, td):
    del xs_in_hbm

    def start(r, carry):
        for k in range(TOP_K):
            d = dest_ref[0, 0, r * TOP_K + k]
            pltpu.make_async_copy(h_ref.at[pl.ds(r, 1)], xs_hbm.at[pl.ds(d, 1)], sem).start()
        return carry

    lax.fori_loop(0, td, start, 0)
    for _ in range(TOP_K):
        pltpu.make_async_copy(h_ref, xs_hbm.at[pl.ds(0, td)], sem).wait()


def _dispatch(h2d, dest, xs0):
    n, d = h2d.shape
    n_slots = xs0.shape[0]
    td = _pick(n, (256, 128))
    dest3 = dest.reshape(n // td, 1, td * TOP_K)
    return pl.pallas_call(
        functools.partial(_dispatch_kernel, td=td),
        out_shape=jax.ShapeDtypeStruct((n_slots, d), h2d.dtype),
        grid=(n // td,),
        in_specs=[
            pl.BlockSpec((1, 1, td * TOP_K), lambda i: (i, 0, 0), memory_space=pltpu.SMEM),
            pl.BlockSpec((td, d), lambda i: (i, 0)),
            pl.BlockSpec(memory_space=pl.ANY),
        ],
        out_specs=pl.BlockSpec(memory_space=pl.ANY),
        scratch_shapes=[pltpu.SemaphoreType.DMA(())],
        input_output_aliases={2: 0},
        compiler_params=_cparams(("arbitrary",)),
        name="dispatch",
    )(dest3, h2d, xs0)


def _expert_kernel(te_ref, nu_ref, x_ref, wg_ref, wl_ref, bg_ref, bl_ref, wd_ref, bd_ref, y_ref,
                   xb_ref, acc_ref, *, nf):
    del te_ref
    i = pl.program_id(0)
    f = pl.program_id(1)
    used = i < nu_ref[0]

    @pl.when(used)
    def _():
        @pl.when(f == 0)
        def _():
            xb_ref[...] = x_ref[...].astype(BF16)
            acc_ref[...] = jnp.zeros_like(acc_ref)

        xb = xb_ref[...]
        g = jnp.dot(xb, wg_ref[0], preferred_element_type=F32) + bg_ref[0]
        lin = jnp.dot(xb, wl_ref[0], preferred_element_type=F32) + bl_ref[0]
        g = jnp.minimum(g, SWIGLU_LIMIT)
        lin = jnp.clip(lin, -SWIGLU_LIMIT, SWIGLU_LIMIT)
        act = g * jax.nn.sigmoid(SWIGLU_ALPHA * g) * (lin + 1.0)
        acc_ref[...] += jnp.dot(act.astype(BF16), wd_ref[0], preferred_element_type=F32)

        @pl.when(f == nf - 1)
        def _():
            y_ref[...] = acc_ref[...] + bd_ref[0]

    @pl.when(jnp.logical_and(jnp.logical_not(used), f == nf - 1))
    def _():
        y_ref[...] = jnp.zeros_like(y_ref)


def _expert_mlp(xs, tile_expert, n_used, w_gu_b, b_gu, w_down_b, b_down, te):
    n_slots, d = xs.shape
    ne, _, d_ff2 = w_gu_b.shape
    d_ff = d_ff2 // 2
    tf = _pick(d_ff, (1024, 512, 256, 128))
    nf = d_ff // tf
    n_tiles = n_slots // te

    def tile(i, nu):
        return jnp.minimum(i, nu[0] - 1)

    def ff(i, f, nu):
        return jnp.where(i < nu[0], f, nf - 1)

    grid_spec = pltpu.PrefetchScalarGridSpec(
        num_scalar_prefetch=2,
        grid=(n_tiles, nf),
        in_specs=[
            pl.BlockSpec((te, d), lambda i, f, tx, nu: (tile(i, nu), 0)),
            pl.BlockSpec((1, d, tf), lambda i, f, tx, nu: (tx[tile(i, nu)], 0, ff(i, f, nu))),
            pl.BlockSpec((1, d, tf), lambda i, f, tx, nu: (tx[tile(i, nu)], 0, nf + ff(i, f, nu))),
            pl.BlockSpec((1, 1, tf), lambda i, f, tx, nu: (tx[tile(i, nu)], 0, ff(i, f, nu))),
            pl.BlockSpec((1, 1, tf), lambda i, f, tx, nu: (tx[tile(i, nu)], 0, nf + ff(i, f, nu))),
            pl.BlockSpec((1, tf, d), lambda i, f, tx, nu: (tx[tile(i, nu)], ff(i, f, nu), 0)),
            pl.BlockSpec((1, 1, d), lambda i, f, tx, nu: (tx[tile(i, nu)], 0, 0)),
        ],
        out_specs=pl.BlockSpec((te, d), lambda i, f, tx, nu: (i, 0)),
        scratch_shapes=[pltpu.VMEM((te, d), BF16), pltpu.VMEM((te, d), F32)],
    )
    return pl.pallas_call(
        functools.partial(_expert_kernel, nf=nf),
        out_shape=jax.ShapeDtypeStruct((n_slots, d), F32),
        grid_spec=grid_spec,
        compiler_params=_cparams(("arbitrary", "arbitrary")),
        name="expert_mlp",
    )(tile_expert, n_used, xs, w_gu_b, w_gu_b, b_gu.reshape(ne, 1, d_ff2), b_gu.reshape(ne, 1, d_ff2),
      w_down_b, b_down.reshape(ne, 1, d))


def _combine_kernel(dest_ref, x1_ref, gw_ref, mod_ref, gf_ref, y_hbm, o_ref, ybuf, sem, *, tc):
    def start(r, carry):
        for k in range(TOP_K):
            d = dest_ref[0, 0, r * TOP_K + k]
            pltpu.make_async_copy(y_hbm.at[pl.ds(d, 1)], ybuf.at[k, pl.ds(r, 1)], sem).start()
        return carry

    lax.fori_loop(0, tc, start, 0)
    for k in range(TOP_K):
        pltpu.make_async_copy(y_hbm.at[pl.ds(0, tc)], ybuf.at[k], sem).wait()

    gw = gw_ref[...]
    moe = gw[:, 0:1] * ybuf[0]
    for k in range(1, TOP_K):
        moe = moe + gw[:, k:k + 1] * ybuf[k]
    gate2 = mod_ref[0, 5:6, :]
    x2 = x1_ref[...] + gate2 * moe
    ms = jnp.mean(x2 * x2, axis=-1, keepdims=True)
    o_ref[...] = x2 * lax.rsqrt(ms + EPS) * gf_ref[...]


def _combine(x1, gate_w, dest, y, mod3, mod_row, final_g):
    n, d = x1.shape
    tc = _pick(n, (128,))
    dest3 = dest.reshape(n // tc, 1, tc * TOP_K)
    return pl.pallas_call(
        functools.partial(_combine_kernel, tc=tc),
        out_shape=jax.ShapeDtypeStruct((n, d), F32),
        grid=(n // tc,),
        in_specs=[
            pl.BlockSpec((1, 1, tc * TOP_K), lambda i: (i, 0, 0), memory_space=pltpu.SMEM),
            pl.BlockSpec((tc, d), lambda i: (i, 0)),
            pl.BlockSpec((tc, TOP_K), lambda i: (i, 0)),
            pl.BlockSpec((1, N_MOD, d), lambda i: (mod_row(i * tc), 0, 0)),
            _resident((1, d), lambda i: (0, 0)),
            pl.BlockSpec(memory_space=pl.ANY),
        ],
        out_specs=pl.BlockSpec((tc, d), lambda i: (i, 0)),
        scratch_shapes=[pltpu.VMEM((TOP_K, tc, d), F32), pltpu.SemaphoreType.DMA(())],
        compiler_params=_cparams(("arbitrary",)),
        name="combine",
    )(dest3, x1, gate_w, mod3, final_g.reshape(1, d), y)


def _rope_tables(seq):
    rows = seq // GRID_W
    row = jnp.repeat(jnp.arange(rows), GRID_W).astype(F32)
    col = jnp.tile(jnp.arange(GRID_W), rows).astype(F32)
    freqs = ROPE_THETA ** (-jnp.arange(ROPE_PAIRS_PER_AXIS, dtype=F32) / ROPE_PAIRS_PER_AXIS)
    ang = jnp.concatenate([row[:, None] * freqs, col[:, None] * freqs], axis=-1)
    cos, sin = jnp.cos(ang), jnp.sin(ang)
    return jnp.concatenate([cos, cos], axis=-1), jnp.concatenate([-sin, sin], axis=-1)


def _moe(groups, wts, final_g, mod3):
    te = EXPERT_TILE_ROWS
    sizes = [g[0].shape[0] for g in groups]
    n = sum(sizes)
    logits = jnp.concatenate([g[2] for g in groups], axis=0)
    ne = logits.shape[1]
    top_i, gate_w, rank, counts = _routing(logits)

    counts = counts.reshape(ne).astype(I32)
    padded = (counts + te - 1) // te * te
    pad_end = jnp.cumsum(padded)
    pad_start = pad_end - padded
    n_tiles = (n * TOP_K) // te + ne
    n_slots = n_tiles * te
    tile_start = jnp.arange(n_tiles, dtype=I32) * te
    tile_expert = jnp.minimum(jnp.sum((pad_end[None, :] <= tile_start[:, None]).astype(I32), axis=1), ne - 1)
    n_used = (pad_end[-1:] // te).astype(I32)
    dest = pad_start[top_i] + rank

    offs = [sum(sizes[:i]) for i in range(len(sizes))]
    xs = jnp.zeros((n_slots, groups[0][1].shape[1]), F32)
    for (_, h2, _, _), o, sz in zip(groups, offs, sizes):
        xs = _dispatch(h2, dest[o:o + sz], xs)
    y = _expert_mlp(xs, tile_expert, n_used, wts["w_gu"], wts["b_gu"], wts["w_down"], wts["b_down"], te)
    return [_combine(x1, gate_w[o:o + sz], dest[o:o + sz], y, mod3, mod_row, final_g)
            for (x1, _, _, mod_row), o, sz in zip(groups, offs, sizes)]


def _mixer(x2d, batch, seq, mod3, mod_row, rope, kv_ctx, wts, emit_kv_f32):
    outs = _in_projection(x2d, seq, mod3, mod_row, wts["norm1_g"], wts["w_in"], wts["q_norm_g"],
                          wts["k_norm_g"], rope, emit_kv_f32)
    h, u, q, k, v = outs[:5]
    d_q = q.shape[1]
    d_kv = k.shape[1]
    q3 = q.reshape(batch, seq, d_q)
    k3 = k.reshape(batch, seq, d_kv)
    v3 = v.reshape(batch, seq, d_kv)
    if kv_ctx is not None:
        k3 = jnp.concatenate([kv_ctx[0], k3], axis=1)
        v3 = jnp.concatenate([kv_ctx[1], v3], axis=1)
    attn = _attention(q3, k3, v3).reshape(batch * seq, d_q)
    pm = _pool_mixer(u, seq, wts["w_pool"], wts["pool_scale"])
    merged = _gated_merge(h, pm, attn, wts["w_gate"], wts["b_gate"], wts["w_a_out"], wts["w_b_out"])
    x1, h2, logits = _out_projection(merged, x2d, mod3, mod_row, wts["w_o"], wts["norm2_g"],
                                     wts["w_router"], wts["b_router"])
    return (x1, h2, logits, mod_row), outs[5:]


def kernel(x_prompt, x_sample, cache_k, cache_v, c, c_ctx, w_mod, b_mod, norm1_g, w_in, q_norm_g, k_norm_g, w_pool, pool_scale, w_a_out, w_b_out, w_gate, b_gate, w_o, norm2_g, w_router, b_router, w_gu, b_gu, w_down, b_down, final_g):
    depth = w_mod.shape[0]
    assert depth == 1, "the final norm is fused into the last combine; one trunk layer is supported"
    bp, sp, d = x_prompt.shape
    bs, ss, _ = x_sample.shape
    l = 0

    n_rows = -(-(bs + 1) // SUBLANES) * SUBLANES
    cvec = jnp.zeros((n_rows, d), F32).at[:bs].set(c).at[bs].set(c_ctx)
    mod3 = _modulation(cvec, w_mod[l], b_mod[l]).reshape(n_rows, N_MOD, d)

    wts = dict(
        norm1_g=norm1_g[l], w_in=w_in[l].astype(BF16), q_norm_g=q_norm_g[l], k_norm_g=k_norm_g[l],
        w_pool=w_pool[l].astype(BF16), pool_scale=pool_scale[l], w_a_out=w_a_out[l].astype(BF16),
        w_b_out=w_b_out[l].astype(BF16), w_gate=w_gate[l].astype(BF16), b_gate=b_gate[l],
        w_o=w_o[l].astype(BF16), norm2_g=norm2_g[l], w_router=w_router[l].astype(BF16), b_router=b_router[l],
        w_gu=w_gu[l].astype(BF16), b_gu=b_gu[l], w_down=w_down[l].astype(BF16), b_down=b_down[l],
    )

    gp, (kf, vf) = _mixer(x_prompt.reshape(bp * sp, d), bp, sp, mod3, lambda t: bs, None, None, wts, True)
    kv_ctx = (cache_k[:, l].reshape(bs, -1, N_KV_HEADS * HEAD_DIM).astype(BF16),
              cache_v[:, l].reshape(bs, -1, N_KV_HEADS * HEAD_DIM).astype(BF16))
    gs, _ = _mixer(x_sample.reshape(bs * ss, d), bs, ss, mod3, lambda t: t // ss, _rope_tables(ss),
                   kv_ctx, wts, False)
    yp, ys = _moe([gp, gs], wts, final_g, mod3)

    y_prompt = yp.reshape(bp, sp, d)
    y_sample = ys.reshape(bs, ss, d)
    new_cache_k = kf.reshape(bp, 1, sp, N_KV_HEADS, HEAD_DIM)
    new_cache_v = vf.reshape(bp, 1, sp, N_KV_HEADS, HEAD_DIM)
    return (y_prompt, y_sample, new_cache_k, new_cache_v)
```

```python
import functools

import jax
import jax.numpy as jnp
from jax import lax
from jax.experimental import pallas as pl
from jax.experimental.pallas import tpu as pltpu

F32 = jnp.float32
BF16 = jnp.bfloat16
I32 = jnp.int32

GRID_W = 64
N_HEADS = 16
N_KV_HEADS = 4
HEAD_DIM = 128
Q_GROUP = N_HEADS // N_KV_HEADS
ROPE_PAIRS_PER_AXIS = HEAD_DIM // 4
ROPE_THETA = 10000.0
POOL_WINDOWS = (2, 4, 8, 16)
TOP_K = 4
SWIGLU_LIMIT = 7.0
SWIGLU_ALPHA = 1.702
N_MOD = 6
EPS = 1e-6

LANES = 128
SUBLANES = 8
BF16_SUBLANES = 16
VMEM_LIMIT_BYTES = 56 * 1024 * 1024
SIDE_BLOCK_BYTES = 4 * 1024 * 1024

EXPERT_TILE_ROWS = 512
POOL_HALO = SUBLANES


def _cparams(semantics):
    return pltpu.CompilerParams(dimension_semantics=semantics, vmem_limit_bytes=VMEM_LIMIT_BYTES)


def _pick(n, candidates):
    for c in candidates:
        if n % c == 0:
            return c
    raise ValueError(f"no tile in {candidates} divides {n}")


def _resident(shape, index_map):
    return pl.BlockSpec(shape, index_map, pipeline_mode=pl.Buffered(1))


def _mod_kernel(c_ref, w_ref, b_ref, o_ref):
    c = c_ref[...]
    s = (c * jax.nn.sigmoid(c)).astype(BF16)
    o_ref[...] = jnp.dot(s, w_ref[...].astype(BF16), preferred_element_type=F32) + b_ref[...]


def _modulation(cvec, w_mod, b_mod):
    r, d = cvec.shape
    n6 = w_mod.shape[1]
    tn = _pick(n6, (1024, 512, 256, 128))
    return pl.pallas_call(
        _mod_kernel,
        out_shape=jax.ShapeDtypeStruct((r, n6), F32),
        grid=(n6 // tn,),
        in_specs=[
            pl.BlockSpec((r, d), lambda j: (0, 0)),
            pl.BlockSpec((d, tn), lambda j: (0, j)),
            pl.BlockSpec((1, tn), lambda j: (0, j)),
        ],
        out_specs=pl.BlockSpec((r, tn), lambda j: (0, j)),
        compiler_params=_cparams(("arbitrary",)),
        name="modulation",
    )(cvec, w_mod, b_mod.reshape(1, n6))


def _inproj_kernel(*refs, use_rope, emit_kv_f32, d_pool):
    it = iter(refs)
    x_ref, mod_ref, g1_ref, w_ref, qg_ref, kg_ref = (next(it) for _ in range(6))
    cos_ref = sin_ref = None
    if use_rope:
        cos_ref, sin_ref = next(it), next(it)
    h_ref, u_ref, q_ref, k_ref, v_ref = (next(it) for _ in range(5))
    kf_ref = vf_ref = None
    if emit_kv_f32:
        kf_ref, vf_ref = next(it), next(it)

    x = x_ref[...]
    ms = jnp.mean(x * x, axis=-1, keepdims=True)
    xn = x * lax.rsqrt(ms + EPS) * g1_ref[...]
    shift = mod_ref[0, 0:1, :]
    scale = mod_ref[0, 1:2, :]
    hb = (xn * (1.0 + scale) + shift).astype(BF16)
    h_ref[...] = hb

    d_q = N_HEADS * HEAD_DIM
    d_kv = N_KV_HEADS * HEAD_DIM
    chunk = Q_GROUP * HEAD_DIM

    for c0 in range(0, d_pool, chunk):
        u_ref[:, c0:c0 + chunk] = jnp.dot(hb, w_ref[:, c0:c0 + chunk], preferred_element_type=F32)

    if use_rope:
        cos = cos_ref[...]
        sin = sin_ref[...]

    def head(xh, g):
        hm = jnp.mean(xh * xh, axis=-1, keepdims=True)
        y = xh * lax.rsqrt(hm + EPS) * g
        if use_rope:
            y = y * cos + pltpu.roll(y, HEAD_DIM // 2, axis=1) * sin
        return y

    qg = qg_ref[...]
    kg = kg_ref[...]
    for c0 in range(0, d_q, chunk):
        pr = jnp.dot(hb, w_ref[:, d_pool + c0:d_pool + c0 + chunk], preferred_element_type=F32)
        for j in range(chunk // HEAD_DIM):
            yh = head(pr[:, j * HEAD_DIM:(j + 1) * HEAD_DIM], qg)
            q_ref[:, c0 + j * HEAD_DIM:c0 + (j + 1) * HEAD_DIM] = yh.astype(BF16)

    pk = jnp.dot(hb, w_ref[:, d_pool + d_q:d_pool + d_q + d_kv], preferred_element_type=F32)
    for j in range(N_KV_HEADS):
        yh = head(pk[:, j * HEAD_DIM:(j + 1) * HEAD_DIM], kg)
        k_ref[:, j * HEAD_DIM:(j + 1) * HEAD_DIM] = yh.astype(BF16)
        if emit_kv_f32:
            kf_ref[:, j * HEAD_DIM:(j + 1) * HEAD_DIM] = yh

    pv = jnp.dot(hb, w_ref[:, d_pool + d_q + d_kv:], preferred_element_type=F32)
    v_ref[...] = pv.astype(BF16)
    if emit_kv_f32:
        vf_ref[...] = pv


def _in_projection(x2d, seq, mod3, mod_row, g1, w_in_b, qg, kg, rope, emit_kv_f32):
    n, d = x2d.shape
    d_in = w_in_b.shape[1]
    d_q = N_HEADS * HEAD_DIM
    d_kv = N_KV_HEADS * HEAD_DIM
    d_pool = d_in - d_q - 2 * d_kv
    ts = _pick(seq, (512, 256, 128))
    tiles_per_seq = seq // ts
    use_rope = rope is not None

    in_specs = [
        pl.BlockSpec((ts, d), lambda i: (i, 0)),
        pl.BlockSpec((1, N_MOD, d), lambda i: (mod_row(i * ts), 0, 0)),
        _resident((1, d), lambda i: (0, 0)),
        _resident((d, d_in), lambda i: (0, 0)),
        _resident((1, HEAD_DIM), lambda i: (0, 0)),
        _resident((1, HEAD_DIM), lambda i: (0, 0)),
    ]
    args = [x2d, mod3, g1.reshape(1, d), w_in_b, qg.reshape(1, HEAD_DIM), kg.reshape(1, HEAD_DIM)]
    if use_rope:
        in_specs += [pl.BlockSpec((ts, HEAD_DIM), lambda i: (i % tiles_per_seq, 0))] * 2
        args += list(rope)

    out_shape = [
        jax.ShapeDtypeStruct((n, d), BF16),
        jax.ShapeDtypeStruct((n, d_pool), F32),
        jax.ShapeDtypeStruct((n, d_q), BF16),
        jax.ShapeDtypeStruct((n, d_kv), BF16),
        jax.ShapeDtypeStruct((n, d_kv), BF16),
    ]
    out_specs = [
        pl.BlockSpec((ts, d), lambda i: (i, 0)),
        pl.BlockSpec((ts, d_pool), lambda i: (i, 0)),
        pl.BlockSpec((ts, d_q), lambda i: (i, 0)),
        pl.BlockSpec((ts, d_kv), lambda i: (i, 0)),
        pl.BlockSpec((ts, d_kv), lambda i: (i, 0)),
    ]
    if emit_kv_f32:
        out_shape += [jax.ShapeDtypeStruct((n, d_kv), F32)] * 2
        out_specs += [pl.BlockSpec((ts, d_kv), lambda i: (i, 0))] * 2

    return pl.pallas_call(
        functools.partial(_inproj_kernel, use_rope=use_rope, emit_kv_f32=emit_kv_f32, d_pool=d_pool),
        out_shape=out_shape,
        grid=(n // ts,),
        in_specs=in_specs,
        out_specs=out_specs,
        compiler_params=_cparams(("arbitrary",)),
        name="in_projection",
    )(*args)


def _attn_kernel(*refs, tk, n_chunks, scale, n_cast, n_zero):
    q_ref, k_ref, v_ref = refs[:3]
    cast_in = refs[3:3 + n_cast]
    o_ref = refs[3 + n_cast]
    cast_out = refs[4 + n_cast:4 + 2 * n_cast]
    zero_out = refs[4 + 2 * n_cast:4 + 2 * n_cast + n_zero]
    for src, dst in zip(cast_in, cast_out):
        dst[...] = src[...].astype(dst.dtype)
    for dst in zero_out:
        dst[...] = jnp.zeros_like(dst)

    q = q_ref[0]
    tq = q.shape[0]
    qs = jnp.concatenate([q[:, j * HEAD_DIM:(j + 1) * HEAD_DIM] for j in range(Q_GROUP)], axis=0)
    rows = qs.shape[0]
    c2 = scale * 1.4426950408889634
    m = jnp.full((rows, 1), -jnp.inf, F32)
    l = jnp.zeros((rows, 1), F32)
    acc = jnp.zeros((rows, HEAD_DIM), F32)
    for c in range(n_chunks):
        kc = k_ref[0, c * tk:(c + 1) * tk, :]
        vc = v_ref[0, c * tk:(c + 1) * tk, :]
        s = lax.dot_general(qs, kc, (((1,), (1,)), ((), ())), preferred_element_type=F32) * c2
        m_new = jnp.maximum(m, jnp.max(s, axis=1, keepdims=True))
        alpha = jnp.exp2(m - m_new)
        p = jnp.exp2(s - m_new)
        l = alpha * l + jnp.sum(p, axis=1, keepdims=True)
        acc = alpha * acc + jnp.dot(p.astype(BF16), vc, preferred_element_type=F32)
        m = m_new
    out = acc / l
    o_ref[0] = jnp.concatenate([out[j * tq:(j + 1) * tq] for j in range(Q_GROUP)], axis=1).astype(o_ref.dtype)


def _side_rows(total_rows, row_bytes, n_steps):
    if total_rows % n_steps:
        return None
    rows = total_rows // n_steps
    if rows % BF16_SUBLANES or rows * row_bytes > SIDE_BLOCK_BYTES:
        return None
    return rows


def _attention(q3, k3, v3, cast_jobs=(), zero_jobs=()):
    b, s, d_q = q3.shape
    kv_len = k3.shape[1]
    tq = _pick(s, (256, 128))
    tk = _pick(kv_len, (256, 128))
    gw = Q_GROUP * HEAD_DIM
    nq = s // tq
    n_steps = b * N_KV_HEADS * nq

    def step(bi, g, qi):
        return ((bi * N_KV_HEADS + g) * nq + qi, 0)

    cast_rows = [_side_rows(a.shape[0], a.shape[1] * 4, n_steps) for a in cast_jobs]
    zero_rows = [_side_rows(r, c * 4, n_steps) for r, c in zero_jobs]
    fused_cast = [a for a, r in zip(cast_jobs, cast_rows) if r is not None]
    fused_zero = [z for z, r in zip(zero_jobs, zero_rows) if r is not None]

    in_specs = [
        pl.BlockSpec((1, tq, gw), lambda bi, g, qi: (bi, qi, g)),
        pl.BlockSpec((1, kv_len, HEAD_DIM), lambda bi, g, qi: (bi, 0, g)),
        pl.BlockSpec((1, kv_len, HEAD_DIM), lambda bi, g, qi: (bi, 0, g)),
    ]
    out_shape = [jax.ShapeDtypeStruct((b, s, d_q), BF16)]
    out_specs = [pl.BlockSpec((1, tq, gw), lambda bi, g, qi: (bi, qi, g))]
    for a, r in zip(cast_jobs, cast_rows):
        if r is not None:
            in_specs.append(pl.BlockSpec((r, a.shape[1]), step))
            out_shape.append(jax.ShapeDtypeStruct(a.shape, BF16))
            out_specs.append(pl.BlockSpec((r, a.shape[1]), step))
    for (rows, cols), r in zip(zero_jobs, zero_rows):
        if r is not None:
            out_shape.append(jax.ShapeDtypeStruct((rows, cols), F32))
            out_specs.append(pl.BlockSpec((r, cols), step))

    outs = pl.pallas_call(
        functools.partial(_attn_kernel, tk=tk, n_chunks=kv_len // tk, scale=HEAD_DIM ** -0.5,
                          n_cast=len(fused_cast), n_zero=len(fused_zero)),
        out_shape=out_shape,
        grid=(b, N_KV_HEADS, nq),
        in_specs=in_specs,
        out_specs=out_specs,
        compiler_params=_cparams(("arbitrary", "arbitrary", "arbitrary")),
        name="attention",
    )(q3, k3, v3, *fused_cast)

    it = iter(outs[1:])
    casts = [next(it) if r is not None else a.astype(BF16) for a, r in zip(cast_jobs, cast_rows)]
    zeros = [next(it) if r is not None else jnp.zeros(z, F32) for z, r in zip(zero_jobs, zero_rows)]
    return outs[0], casts, zeros


def _pool_kernel(up_ref, u_ref, un_ref, wp_ref, ps_ref, o_ref, ext_ref, *, tp, seq):
    i = pl.program_id(0)
    pos0 = (i * tp) % seq
    has_prev = pos0 > 0
    has_next = pos0 + tp < seq
    h = POOL_HALO
    ext_ref[0:h, :] = jnp.where(has_prev, up_ref[...], 0.0)
    ext_ref[h:h + tp, :] = u_ref[...]
    ext_ref[h + tp:h + tp + h, :] = jnp.where(has_next, un_ref[...], 0.0)

    gdim = wp_ref.shape[1]
    pos = pos0 + lax.broadcasted_iota(I32, (tp, 1), 0)
    for g, w in enumerate(POOL_WINDOWS):
        c0 = g * gdim
        acc = jnp.zeros((tp, gdim), F32)
        for j in range(-(w // 2), w - w // 2):
            acc = acc + ext_ref[h + j:h + j + tp, c0:c0 + gdim]
        lo = jnp.maximum(pos - w // 2, 0)
        hi = jnp.minimum(pos + (w - w // 2), seq)
        mean = acc / (hi - lo).astype(F32)
        pooled = (mean - u_ref[:, c0:c0 + gdim]).astype(BF16)
        y = jnp.dot(pooled, wp_ref[g], preferred_element_type=F32) * ps_ref[:, c0:c0 + gdim]
        o_ref[:, c0:c0 + gdim] = y.astype(o_ref.dtype)


def _pool_mixer(u2d, seq, w_pool_b, pool_scale):
    n, d_pool = u2d.shape
    tp = _pick(seq, (512, 256, 128))
    h = POOL_HALO
    assert max(POOL_WINDOWS) // 2 <= h
    nb = n // h
    per = tp // h
    return pl.pallas_call(
        functools.partial(_pool_kernel, tp=tp, seq=seq),
        out_shape=jax.ShapeDtypeStruct((n, d_pool), BF16),
        grid=(n // tp,),
        in_specs=[
            pl.BlockSpec((h, d_pool), lambda i: (jnp.maximum(i * per - 1, 0), 0)),
            pl.BlockSpec((tp, d_pool), lambda i: (i, 0)),
            pl.BlockSpec((h, d_pool), lambda i: (jnp.minimum((i + 1) * per, nb - 1), 0)),
            _resident(w_pool_b.shape, lambda i: (0, 0, 0)),
            _resident((1, d_pool), lambda i: (0, 0)),
        ],
        out_specs=pl.BlockSpec((tp, d_pool), lambda i: (i, 0)),
        scratch_shapes=[pltpu.VMEM((tp + 2 * h, d_pool), F32)],
        compiler_params=_cparams(("arbitrary",)),
        name="pool_mixer",
    )(u2d, u2d, u2d, w_pool_b, pool_scale.reshape(1, d_pool))


def _merge_kernel(h_ref, pm_ref, at_ref, wga_ref, wgb_ref, bga_ref, bgb_ref, wa_ref, wb_ref, o_ref):
    hb = h_ref[...]
    ga = jax.nn.sigmoid(jnp.dot(hb, wga_ref[...], preferred_element_type=F32) + bga_ref[...])
    gb = jax.nn.sigmoid(jnp.dot(hb, wgb_ref[...], preferred_element_type=F32) + bgb_ref[...])
    ya = jnp.dot(pm_ref[...], wa_ref[...], preferred_element_type=F32)
    yb = jnp.dot(at_ref[...], wb_ref[...], preferred_element_type=F32)
    o_ref[...] = (ga * ya + gb * yb).astype(o_ref.dtype)


def _gated_merge(h2d, pm2d, at2d, w_gate_b, b_gate, w_a_b, w_b_b):
    n, d = h2d.shape
    d_pool = pm2d.shape[1]
    d_q = at2d.shape[1]
    tm = _pick(n, (512, 256, 128))
    tn = _pick(d, (512, 256, 128))
    nj = d // tn
    bg = b_gate.reshape(1, 2 * d)
    return pl.pallas_call(
        _merge_kernel,
        out_shape=jax.ShapeDtypeStruct((n, d), BF16),
        grid=(n // tm, nj),
        in_specs=[
            pl.BlockSpec((tm, d), lambda i, j: (i, 0)),
            pl.BlockSpec((tm, d_pool), lambda i, j: (i, 0)),
            pl.BlockSpec((tm, d_q), lambda i, j: (i, 0)),
            pl.BlockSpec((d, tn), lambda i, j: (0, j)),
            pl.BlockSpec((d, tn), lambda i, j: (0, nj + j)),
            pl.BlockSpec((1, tn), lambda i, j: (0, j)),
            pl.BlockSpec((1, tn), lambda i, j: (0, nj + j)),
            pl.BlockSpec((d_pool, tn), lambda i, j: (0, j)),
            pl.BlockSpec((d_q, tn), lambda i, j: (0, j)),
        ],
        out_specs=pl.BlockSpec((tm, tn), lambda i, j: (i, j)),
        compiler_params=_cparams(("arbitrary", "arbitrary")),
        name="gated_merge",
    )(h2d, pm2d, at2d, w_gate_b, w_gate_b, bg, bg, w_a_b, w_b_b)


def _outproj_kernel(mg_ref, x_ref, mod_ref, wo_ref, g2_ref, wr_ref, br_ref, x1_ref, h2_ref, lg_ref):
    mix = jnp.dot(mg_ref[...], wo_ref[...], preferred_element_type=F32)
    gate1 = mod_ref[0, 2:3, :]
    shift2 = mod_ref[0, 3:4, :]
    scale2 = mod_ref[0, 4:5, :]
    x1 = x_ref[...] + gate1 * mix
    x1_ref[...] = x1
    ms = jnp.mean(x1 * x1, axis=-1, keepdims=True)
    h2 = (x1 * lax.rsqrt(ms + EPS) * g2_ref[...]) * (1.0 + scale2) + shift2
    h2_ref[...] = h2
    lg_ref[...] = jnp.dot(h2.astype(BF16), wr_ref[...], preferred_element_type=F32) + br_ref[...]


def _out_projection(mg2d, x2d, mod3, mod_row, w_o_b, g2, w_router, b_router):
    n, d = x2d.shape
    ne = w_router.shape[1]
    tm = _pick(n, (256, 128))
    return pl.pallas_call(
        _outproj_kernel,
        out_shape=[
            jax.ShapeDtypeStruct((n, d), F32),
            jax.ShapeDtypeStruct((n, d), F32),
            jax.ShapeDtypeStruct((n, ne), F32),
        ],
        grid=(n // tm,),
        in_specs=[
            pl.BlockSpec((tm, d), lambda i: (i, 0)),
            pl.BlockSpec((tm, d), lambda i: (i, 0)),
            pl.BlockSpec((1, N_MOD, d), lambda i: (mod_row(i * tm), 0, 0)),
            _resident((d, d), lambda i: (0, 0)),
            _resident((1, d), lambda i: (0, 0)),
            _resident((d, ne), lambda i: (0, 0)),
            _resident((1, ne), lambda i: (0, 0)),
        ],
        out_specs=[
            pl.BlockSpec((tm, d), lambda i: (i, 0)),
            pl.BlockSpec((tm, d), lambda i: (i, 0)),
            pl.BlockSpec((tm, ne), lambda i: (i, 0)),
        ],
        compiler_params=_cparams(("arbitrary",)),
        name="out_projection",
    )(mg2d, x2d, mod3, w_o_b, g2.reshape(1, d), w_router, b_router.reshape(1, ne))


def _route_kernel(lg_ref, idx_ref, gw_ref, rk_ref, cnt_ref):
    step = pl.program_id(0)

    @pl.when(step == 0)
    def _():
        cnt_ref[...] = jnp.zeros_like(cnt_ref)

    lg = lg_ref[...]
    tr, ne = lg.shape
    lane = lax.broadcasted_iota(I32, (tr, ne), 1)
    work = lg
    vals, idxs = [], []
    member = jnp.zeros((tr, ne), F32)
    for _ in range(TOP_K):
        mx = jnp.max(work, axis=-1, keepdims=True)
        ix = jnp.min(jnp.where(work == mx, lane, ne), axis=-1, keepdims=True)
        sel = lane == ix
        vals.append(mx)
        idxs.append(ix)
        member = jnp.where(sel, 1.0, member)
        work = jnp.where(sel, -jnp.inf, work)

    ex = [jnp.exp(v - vals[0]) for v in vals]
    den = ex[0]
    for e in ex[1:]:
        den = den + e

    row = lax.broadcasted_iota(I32, (tr, tr), 0)
    col = lax.broadcasted_iota(I32, (tr, tr), 1)
    lower = jnp.where(col < row, 1.0, 0.0).astype(BF16)
    before = jnp.dot(lower, member.astype(BF16), preferred_element_type=F32) + cnt_ref[...]

    k_lane = lax.broadcasted_iota(I32, (tr, TOP_K), 1)
    idx_out = jnp.zeros((tr, TOP_K), I32)
    gw_out = jnp.zeros((tr, TOP_K), F32)
    rk_out = jnp.zeros((tr, TOP_K), I32)
    for k in range(TOP_K):
        rk = jnp.sum(jnp.where(lane == idxs[k], before, 0.0), axis=-1, keepdims=True).astype(I32)
        idx_out = jnp.where(k_lane == k, idxs[k], idx_out)
        gw_out = jnp.where(k_lane == k, ex[k] / den, gw_out)
        rk_out = jnp.where(k_lane == k, rk, rk_out)
    idx_ref[...] = idx_out
    gw_ref[...] = gw_out
    rk_ref[...] = rk_out
    cnt_ref[...] += jnp.sum(member, axis=0, keepdims=True)


def _routing(logits):
    n, ne = logits.shape
    tr = _pick(n, (512, 256, 128))
    return pl.pallas_call(
        _route_kernel,
        out_shape=[
            jax.ShapeDtypeStruct((n, TOP_K), I32),
            jax.ShapeDtypeStruct((n, TOP_K), F32),
            jax.ShapeDtypeStruct((n, TOP_K), I32),
            jax.ShapeDtypeStruct((1, ne), F32),
        ],
        grid=(n // tr,),
        in_specs=[pl.BlockSpec((tr, ne), lambda i: (i, 0))],
        out_specs=[
            pl.BlockSpec((tr, TOP_K), lambda i: (i, 0)),
            pl.BlockSpec((tr, TOP_K), lambda i: (i, 0)),
            pl.BlockSpec((tr, TOP_K), lambda i: (i, 0)),
            pl.BlockSpec((1, ne), lambda i: (0, 0)),
        ],
        compiler_params=_cparams(("arbitrary",)),
        name="routing",
    )(logits)


def _dispatch_kernel(dest_ref, h_ref, xs_in_hbm, xs_hbm, sem, *, td):
    del xs_in_hbm

    def start(r, carry):
        for k in range(TOP_K):
            d = dest_ref[0, 0, r * TOP_K + k]
            pltpu.make_async_copy(h_ref.at[pl.ds(r, 1)], xs_hbm.at[pl.ds(d, 1)], sem).start(priority=k % 2)
        return carry

    lax.fori_loop(0, td, start, 0)
    for _ in range(TOP_K):
        pltpu.make_async_copy(h_ref, xs_hbm.at[pl.ds(0, td)], sem).wait()


def _dispatch(h2d, dest, xs0):
    n, d = h2d.shape
    n_slots = xs0.shape[0]
    td = _pick(n, (256, 128))
    dest3 = dest.reshape(n // td, 1, td * TOP_K)
    return pl.pallas_call(
        functools.partial(_dispatch_kernel, td=td),
        out_shape=jax.ShapeDtypeStruct((n_slots, d), h2d.dtype),
        grid=(n // td,),
        in_specs=[
            pl.BlockSpec((1, 1, td * TOP_K), lambda i: (i, 0, 0), memory_space=pltpu.SMEM),
            pl.BlockSpec((td, d), lambda i: (i, 0)),
            pl.BlockSpec(memory_space=pl.ANY),
        ],
        out_specs=pl.BlockSpec(memory_space=pl.ANY),
        scratch_shapes=[pltpu.SemaphoreType.DMA(())],
        input_output_aliases={2: 0},
        compiler_params=_cparams(("arbitrary",)),
        name="dispatch",
    )(dest3, h2d, xs0)


def _expert_kernel(te_ref, nu_ref, x_ref, wg_ref, wl_ref, bg_ref, bl_ref, wd_ref, bd_ref, y_ref, *, nf):
    del te_ref
    i = pl.program_id(0)
    f = pl.program_id(1)
    used = i < nu_ref[0]

    @pl.when(jnp.logical_and(used, f == 0))
    def _():
        y_ref[...] = jnp.broadcast_to(bd_ref[0], y_ref.shape)

    @pl.when(used)
    def _():
        xb = x_ref[...].astype(BF16)
        g = jnp.dot(xb, wg_ref[0], preferred_element_type=F32) + bg_ref[0]
        lin = jnp.dot(xb, wl_ref[0], preferred_element_type=F32) + bl_ref[0]
        g = jnp.minimum(g, SWIGLU_LIMIT)
        lin = jnp.clip(lin, -SWIGLU_LIMIT, SWIGLU_LIMIT)
        act = g * jax.nn.sigmoid(SWIGLU_ALPHA * g) * (lin + 1.0)
        y_ref[...] += jnp.dot(act.astype(BF16), wd_ref[0], preferred_element_type=F32)

    @pl.when(jnp.logical_and(jnp.logical_not(used), f == nf - 1))
    def _():
        y_ref[...] = jnp.zeros_like(y_ref)


def _expert_mlp(xs, tile_expert, n_used, w_gu_b, b_gu, w_down_b, b_down, te):
    n_slots, d = xs.shape
    ne, _, d_ff2 = w_gu_b.shape
    d_ff = d_ff2 // 2
    tf = _pick(d_ff, (1024, 512, 256, 128))
    nf = d_ff // tf
    n_tiles = n_slots // te

    def tile(i, nu):
        return jnp.minimum(i, nu[0] - 1)

    def ff(i, f, nu):
        return jnp.where(i < nu[0], f, nf - 1)

    grid_spec = pltpu.PrefetchScalarGridSpec(
        num_scalar_prefetch=2,
        grid=(n_tiles, nf),
        in_specs=[
            pl.BlockSpec((te, d), lambda i, f, tx, nu: (tile(i, nu), 0)),
            pl.BlockSpec((1, d, tf), lambda i, f, tx, nu: (tx[tile(i, nu)], 0, ff(i, f, nu))),
            pl.BlockSpec((1, d, tf), lambda i, f, tx, nu: (tx[tile(i, nu)], 0, nf + ff(i, f, nu))),
            pl.BlockSpec((1, 1, tf), lambda i, f, tx, nu: (tx[tile(i, nu)], 0, ff(i, f, nu))),
            pl.BlockSpec((1, 1, tf), lambda i, f, tx, nu: (tx[tile(i, nu)], 0, nf + ff(i, f, nu))),
            pl.BlockSpec((1, tf, d), lambda i, f, tx, nu: (tx[tile(i, nu)], ff(i, f, nu), 0)),
            pl.BlockSpec((1, 1, d), lambda i, f, tx, nu: (tx[tile(i, nu)], 0, 0)),
        ],
        out_specs=pl.BlockSpec((te, d), lambda i, f, tx, nu: (i, 0)),
    )
    return pl.pallas_call(
        functools.partial(_expert_kernel, nf=nf),
        out_shape=jax.ShapeDtypeStruct((n_slots, d), F32),
        grid_spec=grid_spec,
        compiler_params=_cparams(("arbitrary", "arbitrary")),
        name="expert_mlp",
    )(tile_expert, n_used, xs, w_gu_b, w_gu_b, b_gu.reshape(ne, 1, d_ff2), b_gu.reshape(ne, 1, d_ff2),
      w_down_b, b_down.reshape(ne, 1, d))


def _combine_kernel(dest_ref, x1_ref, gw_ref, mod_ref, gf_ref, y_hbm, o_ref, ybuf, sem, *, tc):
    def start(r, carry):
        for k in range(TOP_K):
            d = dest_ref[0, 0, r * TOP_K + k]
            pltpu.make_async_copy(y_hbm.at[pl.ds(d, 1)], ybuf.at[k, pl.ds(r, 1)], sem).start(priority=k % 2)
        return carry

    lax.fori_loop(0, tc, start, 0)
    for k in range(TOP_K):
        pltpu.make_async_copy(y_hbm.at[pl.ds(0, tc)], ybuf.at[k], sem).wait()

    gw = gw_ref[...]
    moe = gw[:, 0:1] * ybuf[0]
    for k in range(1, TOP_K):
        moe = moe + gw[:, k:k + 1] * ybuf[k]
    gate2 = mod_ref[0, 5:6, :]
    x2 = x1_ref[...] + gate2 * moe
    ms = jnp.mean(x2 * x2, axis=-1, keepdims=True)
    o_ref[...] = x2 * lax.rsqrt(ms + EPS) * gf_ref[...]


def _combine(x1, gate_w, dest, y, mod3, mod_row, final_g):
    n, d = x1.shape
    tc = _pick(n, (128,))
    dest3 = dest.reshape(n // tc, 1, tc * TOP_K)
    return pl.pallas_call(
        functools.partial(_combine_kernel, tc=tc),
        out_shape=jax.ShapeDtypeStruct((n, d), F32),
        grid=(n // tc,),
        in_specs=[
            pl.BlockSpec((1, 1, tc * TOP_K), lambda i: (i, 0, 0), memory_space=pltpu.SMEM),
            pl.BlockSpec((tc, d), lambda i: (i, 0)),
            pl.BlockSpec((tc, TOP_K), lambda i: (i, 0)),
            pl.BlockSpec((1, N_MOD, d), lambda i: (mod_row(i * tc), 0, 0)),
            _resident((1, d), lambda i: (0, 0)),
            pl.BlockSpec(memory_space=pl.ANY),
        ],
        out_specs=pl.BlockSpec((tc, d), lambda i: (i, 0)),
        scratch_shapes=[pltpu.VMEM((TOP_K, tc, d), F32), pltpu.SemaphoreType.DMA(())],
        compiler_params=_cparams(("arbitrary",)),
        name="combine",
    )(dest3, x1, gate_w, mod3, final_g.reshape(1, d), y)


def _rope_tables(seq):
    rows = seq // GRID_W
    row = jnp.repeat(jnp.arange(rows), GRID_W).astype(F32)
    col = jnp.tile(jnp.arange(GRID_W), rows).astype(F32)
    freqs = ROPE_THETA ** (-jnp.arange(ROPE_PAIRS_PER_AXIS, dtype=F32) / ROPE_PAIRS_PER_AXIS)
    ang = jnp.concatenate([row[:, None] * freqs, col[:, None] * freqs], axis=-1)
    cos, sin = jnp.cos(ang), jnp.sin(ang)
    return jnp.concatenate([cos, cos], axis=-1), jnp.concatenate([-sin, sin], axis=-1)


def _slot_tiles(n_tokens, n_experts):
    return (n_tokens * TOP_K) // EXPERT_TILE_ROWS + n_experts


def _moe(groups, wts, final_g, mod3, xs0):
    te = EXPERT_TILE_ROWS
    sizes = [g[0].shape[0] for g in groups]
    n = sum(sizes)
    logits = jnp.concatenate([g[2] for g in groups], axis=0)
    ne = logits.shape[1]
    top_i, gate_w, rank, counts = _routing(logits)

    counts = counts.reshape(ne).astype(I32)
    padded = (counts + te - 1) // te * te
    pad_end = jnp.cumsum(padded)
    pad_start = pad_end - padded
    n_tiles = _slot_tiles(n, ne)
    assert xs0.shape[0] == n_tiles * te
    tile_start = jnp.arange(n_tiles, dtype=I32) * te
    tile_expert = jnp.minimum(jnp.sum((pad_end[None, :] <= tile_start[:, None]).astype(I32), axis=1), ne - 1)
    n_used = (pad_end[-1:] // te).astype(I32)
    dest = pad_start[top_i] + rank

    offs = [sum(sizes[:i]) for i in range(len(sizes))]
    xs = xs0
    for (_, h2, _, _), o, sz in zip(groups, offs, sizes):
        xs = _dispatch(h2, dest[o:o + sz], xs)
    y = _expert_mlp(xs, tile_expert, n_used, wts["w_gu"], wts["b_gu"], wts["w_down"], wts["b_down"], te)
    return [_combine(x1, gate_w[o:o + sz], dest[o:o + sz], y, mod3, mod_row, final_g)
            for (x1, _, _, mod_row), o, sz in zip(groups, offs, sizes)]


def _mixer(x2d, batch, seq, mod3, mod_row, rope, kv_ctx, wts, emit_kv_f32, cast_jobs=(), zero_jobs=()):
    outs = _in_projection(x2d, seq, mod3, mod_row, wts["norm1_g"], wts["w_in"], wts["q_norm_g"],
                          wts["k_norm_g"], rope, emit_kv_f32)
    h, u, q, k, v = outs[:5]
    d_q = q.shape[1]
    d_kv = k.shape[1]
    q3 = q.reshape(batch, seq, d_q)
    k3 = k.reshape(batch, seq, d_kv)
    v3 = v.reshape(batch, seq, d_kv)
    if kv_ctx is not None:
        k3 = jnp.concatenate([kv_ctx[0], k3], axis=1)
        v3 = jnp.concatenate([kv_ctx[1], v3], axis=1)
    attn, casts, zeros = _attention(q3, k3, v3, cast_jobs, zero_jobs)
    attn = attn.reshape(batch * seq, d_q)
    pm = _pool_mixer(u, seq, wts["w_pool"], wts["pool_scale"])
    merged = _gated_merge(h, pm, attn, wts["w_gate"], wts["b_gate"], wts["w_a_out"], wts["w_b_out"])
    x1, h2, logits = _out_projection(merged, x2d, mod3, mod_row, wts["w_o"], wts["norm2_g"],
                                     wts["w_router"], wts["b_router"])
    return (x1, h2, logits, mod_row), outs[5:], casts, zeros


def kernel(x_prompt, x_sample, cache_k, cache_v, c, c_ctx, w_mod, b_mod, norm1_g, w_in, q_norm_g, k_norm_g, w_pool, pool_scale, w_a_out, w_b_out, w_gate, b_gate, w_o, norm2_g, w_router, b_router, w_gu, b_gu, w_down, b_down, final_g):
    depth = w_mod.shape[0]
    assert depth == 1, "the final norm is fused into the last combine; one trunk layer is supported"
    bp, sp, d = x_prompt.shape
    bs, ss, _ = x_sample.shape
    l = 0

    n_rows = -(-(bs + 1) // SUBLANES) * SUBLANES
    cvec = jnp.zeros((n_rows, d), F32).at[:bs].set(c).at[bs].set(c_ctx)
    mod3 = _modulation(cvec, w_mod[l], b_mod[l]).reshape(n_rows, N_MOD, d)

    wts = dict(
        norm1_g=norm1_g[l], w_in=w_in[l].astype(BF16), q_norm_g=q_norm_g[l], k_norm_g=k_norm_g[l],
        w_pool=w_pool[l].astype(BF16), pool_scale=pool_scale[l], w_a_out=w_a_out[l].astype(BF16),
        w_b_out=w_b_out[l].astype(BF16), w_gate=w_gate[l].astype(BF16), b_gate=b_gate[l],
        w_o=w_o[l].astype(BF16), norm2_g=norm2_g[l], w_router=w_router[l].astype(BF16), b_router=b_router[l],
        b_gu=b_gu[l], b_down=b_down[l],
    )

    gp, (kf, vf), _, _ = _mixer(x_prompt.reshape(bp * sp, d), bp, sp, mod3, lambda t: bs, None, None,
                                wts, True)
    kv_ctx = (cache_k[:, l].reshape(bs, -1, N_KV_HEADS * HEAD_DIM).astype(BF16),
              cache_v[:, l].reshape(bs, -1, N_KV_HEADS * HEAD_DIM).astype(BF16))
    ne, _, d_ff2 = w_gu[l].shape
    n_slots = _slot_tiles(bp * sp + bs * ss, ne) * EXPERT_TILE_ROWS
    gs, _, (w_gu_b, w_down_b), (xs0,) = _mixer(
        x_sample.reshape(bs * ss, d), bs, ss, mod3, lambda t: t // ss, _rope_tables(ss), kv_ctx, wts, False,
        cast_jobs=(w_gu[l].reshape(ne * d, d_ff2), w_down[l].reshape(-1, d)), zero_jobs=((n_slots, d),))
    wts["w_gu"] = w_gu_b.reshape(ne, d, d_ff2)
    wts["w_down"] = w_down_b.reshape(w_down[l].shape)
    yp, ys = _moe([gp, gs], wts, final_g, mod3, xs0)

    y_prompt = yp.reshape(bp, sp, d)
    y_sample = ys.reshape(bs, ss, d)
    new_cache_k = kf.reshape(bp, 1, sp, N_KV_HEADS, HEAD_DIM)
    new_cache_v = vf.reshape(bp, 1, sp, N_KV_HEADS, HEAD_DIM)
    return (y_prompt, y_sample, new_cache_k, new_cache_v)
```

```python
import functools

import jax
import jax.numpy as jnp
from jax import lax
from jax.experimental import pallas as pl
from jax.experimental.pallas import tpu as pltpu

F32 = jnp.float32
BF16 = jnp.bfloat16
I32 = jnp.int32

GRID_W = 64
N_HEADS = 16
N_KV_HEADS = 4
HEAD_DIM = 128
Q_GROUP = N_HEADS // N_KV_HEADS
ROPE_PAIRS_PER_AXIS = HEAD_DIM // 4
ROPE_THETA = 10000.0
POOL_WINDOWS = (2, 4, 8, 16)
TOP_K = 4
SWIGLU_LIMIT = 7.0
SWIGLU_ALPHA = 1.702
N_MOD = 6
EPS = 1e-6

LANES = 128
SUBLANES = 8
BF16_SUBLANES = 16
VMEM_LIMIT_BYTES = 56 * 1024 * 1024
SIDE_BLOCK_BYTES = 4 * 1024 * 1024

EXPERT_TILE_ROWS = 512
POOL_HALO = SUBLANES


def _cparams(semantics):
    return pltpu.CompilerParams(dimension_semantics=semantics, vmem_limit_bytes=VMEM_LIMIT_BYTES)


def _pick(n, candidates):
    for c in candidates:
        if n % c == 0:
            return c
    raise ValueError(f"no tile in {candidates} divides {n}")


def _resident(shape, index_map):
    return pl.BlockSpec(shape, index_map, pipeline_mode=pl.Buffered(1))


def _mod_kernel(c_ref, w_ref, b_ref, o_ref):
    c = c_ref[...]
    s = (c * jax.nn.sigmoid(c)).astype(BF16)
    o_ref[...] = jnp.dot(s, w_ref[...].astype(BF16), preferred_element_type=F32) + b_ref[...]


def _modulation(cvec, w_mod, b_mod):
    r, d = cvec.shape
    n6 = w_mod.shape[1]
    tn = _pick(n6, (1024, 512, 256, 128))
    return pl.pallas_call(
        _mod_kernel,
        out_shape=jax.ShapeDtypeStruct((r, n6), F32),
        grid=(n6 // tn,),
        in_specs=[
            pl.BlockSpec((r, d), lambda j: (0, 0)),
            pl.BlockSpec((d, tn), lambda j: (0, j)),
            pl.BlockSpec((1, tn), lambda j: (0, j)),
        ],
        out_specs=pl.BlockSpec((r, tn), lambda j: (0, j)),
        compiler_params=_cparams(("arbitrary",)),
        name="modulation",
    )(cvec, w_mod, b_mod.reshape(1, n6))


def _inproj_kernel(*refs, use_rope, emit_kv_f32, d_pool):
    it = iter(refs)
    x_ref, mod_ref, g1_ref, w_ref, qg_ref, kg_ref = (next(it) for _ in range(6))
    cos_ref = sin_ref = None
    if use_rope:
        cos_ref, sin_ref = next(it), next(it)
    h_ref, u_ref, q_ref, k_ref, v_ref = (next(it) for _ in range(5))
    kf_ref = vf_ref = None
    if emit_kv_f32:
        kf_ref, vf_ref = next(it), next(it)

    x = x_ref[...]
    ms = jnp.mean(x * x, axis=-1, keepdims=True)
    xn = x * lax.rsqrt(ms + EPS) * g1_ref[...]
    shift = mod_ref[0, 0:1, :]
    scale = mod_ref[0, 1:2, :]
    hb = (xn * (1.0 + scale) + shift).astype(BF16)
    h_ref[...] = hb

    d_q = N_HEADS * HEAD_DIM
    d_kv = N_KV_HEADS * HEAD_DIM
    chunk = Q_GROUP * HEAD_DIM

    for c0 in range(0, d_pool, chunk):
        u_ref[:, c0:c0 + chunk] = jnp.dot(hb, w_ref[:, c0:c0 + chunk], preferred_element_type=F32)

    if use_rope:
        cos = cos_ref[...]
        sin = sin_ref[...]

    def head(xh, g):
        hm = jnp.mean(xh * xh, axis=-1, keepdims=True)
        y = xh * lax.rsqrt(hm + EPS) * g
        if use_rope:
            y = y * cos + pltpu.roll(y, HEAD_DIM // 2, axis=1) * sin
        return y

    qg = qg_ref[...]
    kg = kg_ref[...]
    for c0 in range(0, d_q, chunk):
        pr = jnp.dot(hb, w_ref[:, d_pool + c0:d_pool + c0 + chunk], preferred_element_type=F32)
        for j in range(chunk // HEAD_DIM):
            yh = head(pr[:, j * HEAD_DIM:(j + 1) * HEAD_DIM], qg)
            q_ref[:, c0 + j * HEAD_DIM:c0 + (j + 1) * HEAD_DIM] = yh.astype(BF16)

    pk = jnp.dot(hb, w_ref[:, d_pool + d_q:d_pool + d_q + d_kv], preferred_element_type=F32)
    for j in range(N_KV_HEADS):
        yh = head(pk[:, j * HEAD_DIM:(j + 1) * HEAD_DIM], kg)
        k_ref[:, j * HEAD_DIM:(j + 1) * HEAD_DIM] = yh.astype(BF16)
        if emit_kv_f32:
            kf_ref[:, j * HEAD_DIM:(j + 1) * HEAD_DIM] = yh

    pv = jnp.dot(hb, w_ref[:, d_pool + d_q + d_kv:], preferred_element_type=F32)
    v_ref[...] = pv.astype(BF16)
    if emit_kv_f32:
        vf_ref[...] = pv


def _in_projection(x2d, seq, mod3, mod_row, g1, w_in_b, qg, kg, rope, emit_kv_f32):
    n, d = x2d.shape
    d_in = w_in_b.shape[1]
    d_q = N_HEADS * HEAD_DIM
    d_kv = N_KV_HEADS * HEAD_DIM
    d_pool = d_in - d_q - 2 * d_kv
    ts = _pick(seq, (512, 256, 128))
    tiles_per_seq = seq // ts
    use_rope = rope is not None

    in_specs = [
        pl.BlockSpec((ts, d), lambda i: (i, 0)),
        pl.BlockSpec((1, N_MOD, d), lambda i: (mod_row(i * ts), 0, 0)),
        _resident((1, d), lambda i: (0, 0)),
        _resident((d, d_in), lambda i: (0, 0)),
        _resident((1, HEAD_DIM), lambda i: (0, 0)),
        _resident((1, HEAD_DIM), lambda i: (0, 0)),
    ]
    args = [x2d, mod3, g1.reshape(1, d), w_in_b, qg.reshape(1, HEAD_DIM), kg.reshape(1, HEAD_DIM)]
    if use_rope:
        in_specs += [pl.BlockSpec((ts, HEAD_DIM), lambda i: (i % tiles_per_seq, 0))] * 2
        args += list(rope)

    out_shape = [
        jax.ShapeDtypeStruct((n, d), BF16),
        jax.ShapeDtypeStruct((n, d_pool), F32),
        jax.ShapeDtypeStruct((n, d_q), BF16),
        jax.ShapeDtypeStruct((n, d_kv), BF16),
        jax.ShapeDtypeStruct((n, d_kv), BF16),
    ]
    out_specs = [
        pl.BlockSpec((ts, d), lambda i: (i, 0)),
        pl.BlockSpec((ts, d_pool), lambda i: (i, 0)),
        pl.BlockSpec((ts, d_q), lambda i: (i, 0)),
        pl.BlockSpec((ts, d_kv), lambda i: (i, 0)),
        pl.BlockSpec((ts, d_kv), lambda i: (i, 0)),
    ]
    if emit_kv_f32:
        out_shape += [jax.ShapeDtypeStruct((n, d_kv), F32)] * 2
        out_specs += [pl.BlockSpec((ts, d_kv), lambda i: (i, 0))] * 2

    return pl.pallas_call(
        functools.partial(_inproj_kernel, use_rope=use_rope, emit_kv_f32=emit_kv_f32, d_pool=d_pool),
        out_shape=out_shape,
        grid=(n // ts,),
        in_specs=in_specs,
        out_specs=out_specs,
        compiler_params=_cparams(("arbitrary",)),
        name="in_projection",
    )(*args)


def _attn_kernel(*refs, tk, n_chunks, scale, n_cast, n_zero):
    q_ref, k_ref, v_ref = refs[:3]
    cast_in = refs[3:3 + n_cast]
    o_ref = refs[3 + n_cast]
    cast_out = refs[4 + n_cast:4 + 2 * n_cast]
    zero_out = refs[4 + 2 * n_cast:4 + 2 * n_cast + n_zero]
    for src, dst in zip(cast_in, cast_out):
        dst[...] = src[...].astype(dst.dtype)
    for dst in zero_out:
        dst[...] = jnp.zeros_like(dst)

    q = q_ref[0]
    tq = q.shape[0]
    qs = jnp.concatenate([q[:, j * HEAD_DIM:(j + 1) * HEAD_DIM] for j in range(Q_GROUP)], axis=0)
    rows = qs.shape[0]
    c2 = scale * 1.4426950408889634
    m = jnp.full((rows, 1), -jnp.inf, F32)
    l = jnp.zeros((rows, 1), F32)
    acc = jnp.zeros((rows, HEAD_DIM), F32)
    for c in range(n_chunks):
        kc = k_ref[0, c * tk:(c + 1) * tk, :]
        vc = v_ref[0, c * tk:(c + 1) * tk, :]
        s = lax.dot_general(qs, kc, (((1,), (1,)), ((), ())), preferred_element_type=F32) * c2
        m_new = jnp.maximum(m, jnp.max(s, axis=1, keepdims=True))
        alpha = jnp.exp2(m - m_new)
        p = jnp.exp2(s - m_new)
        l = alpha * l + jnp.sum(p, axis=1, keepdims=True)
        acc = alpha * acc + jnp.dot(p.astype(BF16), vc, preferred_element_type=F32)
        m = m_new
    out = acc / l
    o_ref[0] = jnp.concatenate([out[j * tq:(j + 1) * tq] for j in range(Q_GROUP)], axis=1).astype(o_ref.dtype)


def _side_rows(total_rows, row_bytes, n_steps):
    if total_rows % n_steps:
        return None
    rows = total_rows // n_steps
    if rows % BF16_SUBLANES or rows * row_bytes > SIDE_BLOCK_BYTES:
        return None
    return rows


def _attention(q3, k3, v3, cast_jobs=(), zero_jobs=()):
    b, s, d_q = q3.shape
    kv_len = k3.shape[1]
    tq = _pick(s, (256, 128))
    tk = _pick(kv_len, (256, 128))
    gw = Q_GROUP * HEAD_DIM
    nq = s // tq
    n_steps = b * N_KV_HEADS * nq

    def step(bi, g, qi):
        return ((bi * N_KV_HEADS + g) * nq + qi, 0)

    cast_rows = [_side_rows(a.shape[0], a.shape[1] * 4, n_steps) for a in cast_jobs]
    zero_rows = [_side_rows(r, c * 4, n_steps) for r, c in zero_jobs]
    fused_cast = [a for a, r in zip(cast_jobs, cast_rows) if r is not None]
    fused_zero = [z for z, r in zip(zero_jobs, zero_rows) if r is not None]

    in_specs = [
        pl.BlockSpec((1, tq, gw), lambda bi, g, qi: (bi, qi, g)),
        pl.BlockSpec((1, kv_len, HEAD_DIM), lambda bi, g, qi: (bi, 0, g)),
        pl.BlockSpec((1, kv_len, HEAD_DIM), lambda bi, g, qi: (bi, 0, g)),
    ]
    out_shape = [jax.ShapeDtypeStruct((b, s, d_q), BF16)]
    out_specs = [pl.BlockSpec((1, tq, gw), lambda bi, g, qi: (bi, qi, g))]
    for a, r in zip(cast_jobs, cast_rows):
        if r is not None:
            in_specs.append(pl.BlockSpec((r, a.shape[1]), step))
            out_shape.append(jax.ShapeDtypeStruct(a.shape, BF16))
            out_specs.append(pl.BlockSpec((r, a.shape[1]), step))
    for (rows, cols), r in zip(zero_jobs, zero_rows):
        if r is not None:
            out_shape.append(jax.ShapeDtypeStruct((rows, cols), F32))
            out_specs.append(pl.BlockSpec((r, cols), step))

    outs = pl.pallas_call(
        functools.partial(_attn_kernel, tk=tk, n_chunks=kv_len // tk, scale=HEAD_DIM ** -0.5,
                          n_cast=len(fused_cast), n_zero=len(fused_zero)),
        out_shape=out_shape,
        grid=(b, N_KV_HEADS, nq),
        in_specs=in_specs,
        out_specs=out_specs,
        compiler_params=_cparams(("arbitrary", "arbitrary", "arbitrary")),
        name="attention",
    )(q3, k3, v3, *fused_cast)

    it = iter(outs[1:])
    casts = [next(it) if r is not None else a.astype(BF16) for a, r in zip(cast_jobs, cast_rows)]
    zeros = [next(it) if r is not None else jnp.zeros(z, F32) for z, r in zip(zero_jobs, zero_rows)]
    return outs[0], casts, zeros


def _pool_kernel(up_ref, u_ref, un_ref, wp_ref, ps_ref, o_ref, ext_ref, *, tp, seq):
    i = pl.program_id(0)
    pos0 = (i * tp) % seq
    has_prev = pos0 > 0
    has_next = pos0 + tp < seq
    h = POOL_HALO
    ext_ref[0:h, :] = jnp.where(has_prev, up_ref[...], 0.0)
    ext_ref[h:h + tp, :] = u_ref[...]
    ext_ref[h + tp:h + tp + h, :] = jnp.where(has_next, un_ref[...], 0.0)

    gdim = wp_ref.shape[1]
    pos = pos0 + lax.broadcasted_iota(I32, (tp, 1), 0)
    for g, w in enumerate(POOL_WINDOWS):
        c0 = g * gdim
        acc = jnp.zeros((tp, gdim), F32)
        for j in range(-(w // 2), w - w // 2):
            acc = acc + ext_ref[h + j:h + j + tp, c0:c0 + gdim]
        lo = jnp.maximum(pos - w // 2, 0)
        hi = jnp.minimum(pos + (w - w // 2), seq)
        mean = acc / (hi - lo).astype(F32)
        pooled = (mean - u_ref[:, c0:c0 + gdim]).astype(BF16)
        y = jnp.dot(pooled, wp_ref[g], preferred_element_type=F32) * ps_ref[:, c0:c0 + gdim]
        o_ref[:, c0:c0 + gdim] = y.astype(o_ref.dtype)


def _pool_mixer(u2d, seq, w_pool_b, pool_scale):
    n, d_pool = u2d.shape
    tp = _pick(seq, (512, 256, 128))
    h = POOL_HALO
    assert max(POOL_WINDOWS) // 2 <= h
    nb = n // h
    per = tp // h
    return pl.pallas_call(
        functools.partial(_pool_kernel, tp=tp, seq=seq),
        out_shape=jax.ShapeDtypeStruct((n, d_pool), BF16),
        grid=(n // tp,),
        in_specs=[
            pl.BlockSpec((h, d_pool), lambda i: (jnp.maximum(i * per - 1, 0), 0)),
            pl.BlockSpec((tp, d_pool), lambda i: (i, 0)),
            pl.BlockSpec((h, d_pool), lambda i: (jnp.minimum((i + 1) * per, nb - 1), 0)),
            _resident(w_pool_b.shape, lambda i: (0, 0, 0)),
            _resident((1, d_pool), lambda i: (0, 0)),
        ],
        out_specs=pl.BlockSpec((tp, d_pool), lambda i: (i, 0)),
        scratch_shapes=[pltpu.VMEM((tp + 2 * h, d_pool), F32)],
        compiler_params=_cparams(("arbitrary",)),
        name="pool_mixer",
    )(u2d, u2d, u2d, w_pool_b, pool_scale.reshape(1, d_pool))


def _merge_kernel(h_ref, pm_ref, at_ref, wga_ref, wgb_ref, bga_ref, bgb_ref, wa_ref, wb_ref, o_ref):
    hb = h_ref[...]
    ga = jax.nn.sigmoid(jnp.dot(hb, wga_ref[...], preferred_element_type=F32) + bga_ref[...])
    gb = jax.nn.sigmoid(jnp.dot(hb, wgb_ref[...], preferred_element_type=F32) + bgb_ref[...])
    ya = jnp.dot(pm_ref[...], wa_ref[...], preferred_element_type=F32)
    yb = jnp.dot(at_ref[...], wb_ref[...], preferred_element_type=F32)
    o_ref[...] = (ga * ya + gb * yb).astype(o_ref.dtype)


def _gated_merge(h2d, pm2d, at2d, w_gate_b, b_gate, w_a_b, w_b_b):
    n, d = h2d.shape
    d_pool = pm2d.shape[1]
    d_q = at2d.shape[1]
    tm = _pick(n, (1024, 512, 256, 128))
    tn = _pick(d, (512, 256, 128))
    nj = d // tn
    bg = b_gate.reshape(1, 2 * d)
    return pl.pallas_call(
        _merge_kernel,
        out_shape=jax.ShapeDtypeStruct((n, d), BF16),
        grid=(n // tm, nj),
        in_specs=[
            pl.BlockSpec((tm, d), lambda i, j: (i, 0)),
            pl.BlockSpec((tm, d_pool), lambda i, j: (i, 0)),
            pl.BlockSpec((tm, d_q), lambda i, j: (i, 0)),
            pl.BlockSpec((d, tn), lambda i, j: (0, j)),
            pl.BlockSpec((d, tn), lambda i, j: (0, nj + j)),
            pl.BlockSpec((1, tn), lambda i, j: (0, j)),
            pl.BlockSpec((1, tn), lambda i, j: (0, nj + j)),
            pl.BlockSpec((d_pool, tn), lambda i, j: (0, j)),
            pl.BlockSpec((d_q, tn), lambda i, j: (0, j)),
        ],
        out_specs=pl.BlockSpec((tm, tn), lambda i, j: (i, j)),
        compiler_params=_cparams(("arbitrary", "arbitrary")),
        name="gated_merge",
    )(h2d, pm2d, at2d, w_gate_b, w_gate_b, bg, bg, w_a_b, w_b_b)


def _outproj_kernel(mg_ref, x_ref, mod_ref, wo_ref, g2_ref, wr_ref, br_ref, x1_ref, h2_ref, lg_ref):
    mix = jnp.dot(mg_ref[...], wo_ref[...], preferred_element_type=F32)
    gate1 = mod_ref[0, 2:3, :]
    shift2 = mod_ref[0, 3:4, :]
    scale2 = mod_ref[0, 4:5, :]
    x1 = x_ref[...] + gate1 * mix
    x1_ref[...] = x1
    ms = jnp.mean(x1 * x1, axis=-1, keepdims=True)
    h2 = (x1 * lax.rsqrt(ms + EPS) * g2_ref[...]) * (1.0 + scale2) + shift2
    h2_ref[...] = h2
    lg_ref[...] = jnp.dot(h2.astype(BF16), wr_ref[...], preferred_element_type=F32) + br_ref[...]


def _out_projection(mg2d, x2d, mod3, mod_row, w_o_b, g2, w_router, b_router):
    n, d = x2d.shape
    ne = w_router.shape[1]
    tm = _pick(n, (256, 128))
    return pl.pallas_call(
        _outproj_kernel,
        out_shape=[
            jax.ShapeDtypeStruct((n, d), F32),
            jax.ShapeDtypeStruct((n, d), F32),
            jax.ShapeDtypeStruct((n, ne), F32),
        ],
        grid=(n // tm,),
        in_specs=[
            pl.BlockSpec((tm, d), lambda i: (i, 0)),
            pl.BlockSpec((tm, d), lambda i: (i, 0)),
            pl.BlockSpec((1, N_MOD, d), lambda i: (mod_row(i * tm), 0, 0)),
            _resident((d, d), lambda i: (0, 0)),
            _resident((1, d), lambda i: (0, 0)),
            _resident((d, ne), lambda i: (0, 0)),
            _resident((1, ne), lambda i: (0, 0)),
        ],
        out_specs=[
            pl.BlockSpec((tm, d), lambda i: (i, 0)),
            pl.BlockSpec((tm, d), lambda i: (i, 0)),
            pl.BlockSpec((tm, ne), lambda i: (i, 0)),
        ],
        compiler_params=_cparams(("arbitrary",)),
        name="out_projection",
    )(mg2d, x2d, mod3, w_o_b, g2.reshape(1, d), w_router, b_router.reshape(1, ne))


def _route_kernel(lg_ref, idx_ref, gw_ref, rk_ref, cnt_ref):
    step = pl.program_id(0)

    @pl.when(step == 0)
    def _():
        cnt_ref[...] = jnp.zeros_like(cnt_ref)

    lg = lg_ref[...]
    tr, ne = lg.shape
    lane = lax.broadcasted_iota(I32, (tr, ne), 1)
    work = lg
    vals, idxs = [], []
    member = jnp.zeros((tr, ne), F32)
    for _ in range(TOP_K):
        mx = jnp.max(work, axis=-1, keepdims=True)
        ix = jnp.min(jnp.where(work == mx, lane, ne), axis=-1, keepdims=True)
        sel = lane == ix
        vals.append(mx)
        idxs.append(ix)
        member = jnp.where(sel, 1.0, member)
        work = jnp.where(sel, -jnp.inf, work)

    ex = [jnp.exp(v - vals[0]) for v in vals]
    den = ex[0]
    for e in ex[1:]:
        den = den + e

    row = lax.broadcasted_iota(I32, (tr, tr), 0)
    col = lax.broadcasted_iota(I32, (tr, tr), 1)
    lower = jnp.where(col < row, 1.0, 0.0).astype(BF16)
    before = jnp.dot(lower, member.astype(BF16), preferred_element_type=F32) + cnt_ref[...]

    k_lane = lax.broadcasted_iota(I32, (tr, TOP_K), 1)
    idx_out = jnp.zeros((tr, TOP_K), I32)
    gw_out = jnp.zeros((tr, TOP_K), F32)
    rk_out = jnp.zeros((tr, TOP_K), I32)
    for k in range(TOP_K):
        rk = jnp.sum(jnp.where(lane == idxs[k], before, 0.0), axis=-1, keepdims=True).astype(I32)
        idx_out = jnp.where(k_lane == k, idxs[k], idx_out)
        gw_out = jnp.where(k_lane == k, ex[k] / den, gw_out)
        rk_out = jnp.where(k_lane == k, rk, rk_out)
    idx_ref[...] = idx_out
    gw_ref[...] = gw_out
    rk_ref[...] = rk_out
    cnt_ref[...] += jnp.sum(member, axis=0, keepdims=True)


def _routing(logits):
    n, ne = logits.shape
    tr = _pick(n, (512, 256, 128))
    return pl.pallas_call(
        _route_kernel,
        out_shape=[
            jax.ShapeDtypeStruct((n, TOP_K), I32),
            jax.ShapeDtypeStruct((n, TOP_K), F32),
            jax.ShapeDtypeStruct((n, TOP_K), I32),
            jax.ShapeDtypeStruct((1, ne), F32),
        ],
        grid=(n // tr,),
        in_specs=[pl.BlockSpec((tr, ne), lambda i: (i, 0))],
        out_specs=[
            pl.BlockSpec((tr, TOP_K), lambda i: (i, 0)),
            pl.BlockSpec((tr, TOP_K), lambda i: (i, 0)),
            pl.BlockSpec((tr, TOP_K), lambda i: (i, 0)),
            pl.BlockSpec((1, ne), lambda i: (0, 0)),
        ],
        compiler_params=_cparams(("arbitrary",)),
        name="routing",
    )(logits)


def _dispatch_kernel(dest_ref, h_ref, xs_in_hbm, xs_hbm, sem, *, td):
    del xs_in_hbm

    def start(r, carry):
        for k in range(TOP_K):
            d = dest_ref[0, 0, r * TOP_K + k]
            pltpu.make_async_copy(h_ref.at[pl.ds(r, 1)], xs_hbm.at[pl.ds(d, 1)], sem).start()
        return carry

    lax.fori_loop(0, td, start, 0)
    for _ in range(TOP_K):
        pltpu.make_async_copy(h_ref, xs_hbm.at[pl.ds(0, td)], sem).wait()


def _dispatch(h2d, dest, xs0):
    n, d = h2d.shape
    n_slots = xs0.shape[0]
    td = _pick(n, (256, 128))
    dest3 = dest.reshape(n // td, 1, td * TOP_K)
    return pl.pallas_call(
        functools.partial(_dispatch_kernel, td=td),
        out_shape=jax.ShapeDtypeStruct((n_slots, d), h2d.dtype),
        grid=(n // td,),
        in_specs=[
            pl.BlockSpec((1, 1, td * TOP_K), lambda i: (i, 0, 0), memory_space=pltpu.SMEM),
            pl.BlockSpec((td, d), lambda i: (i, 0)),
            pl.BlockSpec(memory_space=pl.ANY),
        ],
        out_specs=pl.BlockSpec(memory_space=pl.ANY),
        scratch_shapes=[pltpu.SemaphoreType.DMA(())],
        input_output_aliases={2: 0},
        compiler_params=_cparams(("arbitrary",)),
        name="dispatch",
    )(dest3, h2d, xs0)


def _expert_kernel(te_ref, tr_ref, nu_ref, x_ref, wg_ref, wl_ref, bg_ref, bl_ref, wd_ref, bd_ref, y_ref, *, nf):
    del te_ref
    i = pl.program_id(0)
    f = pl.program_id(1)
    used = i < nu_ref[0]
    rows = tr_ref[jnp.minimum(i, nu_ref[0] - 1)]
    te = x_ref.shape[0]
    half = te // 2

    @pl.when(jnp.logical_and(used, f == 0))
    def _():
        y_ref[...] = jnp.broadcast_to(bd_ref[0], y_ref.shape)

    def mlp(m):
        xb = x_ref[0:m, :].astype(BF16)
        g = jnp.dot(xb, wg_ref[0], preferred_element_type=F32) + bg_ref[0]
        lin = jnp.dot(xb, wl_ref[0], preferred_element_type=F32) + bl_ref[0]
        g = jnp.minimum(g, SWIGLU_LIMIT)
        lin = jnp.clip(lin, -SWIGLU_LIMIT, SWIGLU_LIMIT)
        act = g * jax.nn.sigmoid(SWIGLU_ALPHA * g) * (lin + 1.0)
        y_ref[0:m, :] += jnp.dot(act.astype(BF16), wd_ref[0], preferred_element_type=F32)

    @pl.when(jnp.logical_and(used, rows > half))
    def _():
        mlp(te)

    @pl.when(jnp.logical_and(used, rows <= half))
    def _():
        mlp(half)

    @pl.when(jnp.logical_and(jnp.logical_not(used), f == nf - 1))
    def _():
        y_ref[...] = jnp.zeros_like(y_ref)


def _expert_mlp(xs, tile_expert, tile_rows, n_used, w_gu_b, b_gu, w_down_b, b_down, te):
    n_slots, d = xs.shape
    ne, _, d_ff2 = w_gu_b.shape
    d_ff = d_ff2 // 2
    tf = _pick(d_ff, (1024, 512, 256, 128))
    nf = d_ff // tf
    n_tiles = n_slots // te

    def tile(i, nu):
        return jnp.minimum(i, nu[0] - 1)

    def ff(i, f, nu):
        return jnp.where(i < nu[0], f, nf - 1)

    grid_spec = pltpu.PrefetchScalarGridSpec(
        num_scalar_prefetch=3,
        grid=(n_tiles, nf),
        in_specs=[
            pl.BlockSpec((te, d), lambda i, f, tx, tr, nu: (tile(i, nu), 0)),
            pl.BlockSpec((1, d, tf), lambda i, f, tx, tr, nu: (tx[tile(i, nu)], 0, ff(i, f, nu))),
            pl.BlockSpec((1, d, tf), lambda i, f, tx, tr, nu: (tx[tile(i, nu)], 0, nf + ff(i, f, nu))),
            pl.BlockSpec((1, 1, tf), lambda i, f, tx, tr, nu: (tx[tile(i, nu)], 0, ff(i, f, nu))),
            pl.BlockSpec((1, 1, tf), lambda i, f, tx, tr, nu: (tx[tile(i, nu)], 0, nf + ff(i, f, nu))),
            pl.BlockSpec((1, tf, d), lambda i, f, tx, tr, nu: (tx[tile(i, nu)], ff(i, f, nu), 0)),
            pl.BlockSpec((1, 1, d), lambda i, f, tx, tr, nu: (tx[tile(i, nu)], 0, 0)),
        ],
        out_specs=pl.BlockSpec((te, d), lambda i, f, tx, tr, nu: (i, 0)),
    )
    return pl.pallas_call(
        functools.partial(_expert_kernel, nf=nf),
        out_shape=jax.ShapeDtypeStruct((n_slots, d), F32),
        grid_spec=grid_spec,
        compiler_params=_cparams(("arbitrary", "arbitrary")),
        name="expert_mlp",
    )(tile_expert, tile_rows, n_used, xs, w_gu_b, w_gu_b, b_gu.reshape(ne, 1, d_ff2), b_gu.reshape(ne, 1, d_ff2),
      w_down_b, b_down.reshape(ne, 1, d))


def _combine_kernel(dcur_ref, dnext_ref, x1_ref, gw_ref, mod_ref, gf_ref, y_hbm, o_ref, ybuf, sems, *,
                    tc, n_steps):
    i = pl.program_id(0)
    slot = i % 2

    def issue(dest_ref, s):
        def start(r, carry):
            for k in range(TOP_K):
                d = dest_ref[0, 0, r * TOP_K + k]
                pltpu.make_async_copy(y_hbm.at[pl.ds(d, 1)], ybuf.at[s, k, pl.ds(r, 1)], sems.at[s]).start()
            return carry

        lax.fori_loop(0, tc, start, 0)

    @pl.when(i == 0)
    def _():
        issue(dcur_ref, 0)

    for s in range(2):
        @pl.when(jnp.logical_and(i + 1 < n_steps, slot == 1 - s))
        def _():
            issue(dnext_ref, s)

    for k in range(TOP_K):
        pltpu.make_async_copy(y_hbm.at[pl.ds(0, tc)], ybuf.at[slot, k], sems.at[slot]).wait()

    gw = gw_ref[...]
    moe = gw[:, 0:1] * ybuf[slot, 0]
    for k in range(1, TOP_K):
        moe = moe + gw[:, k:k + 1] * ybuf[slot, k]
    gate2 = mod_ref[0, 5:6, :]
    x2 = x1_ref[...] + gate2 * moe
    ms = jnp.mean(x2 * x2, axis=-1, keepdims=True)
    o_ref[...] = x2 * lax.rsqrt(ms + EPS) * gf_ref[...]


def _combine(x1, gate_w, dest, y, mod3, mod_row, final_g):
    n, d = x1.shape
    tc = _pick(n, (128,))
    n_steps = n // tc
    dest3 = dest.reshape(n_steps, 1, tc * TOP_K)
    return pl.pallas_call(
        functools.partial(_combine_kernel, tc=tc, n_steps=n_steps),
        out_shape=jax.ShapeDtypeStruct((n, d), F32),
        grid=(n_steps,),
        in_specs=[
            pl.BlockSpec((1, 1, tc * TOP_K), lambda i: (i, 0, 0), memory_space=pltpu.SMEM),
            pl.BlockSpec((1, 1, tc * TOP_K), lambda i: (jnp.minimum(i + 1, n_steps - 1), 0, 0),
                         memory_space=pltpu.SMEM),
            pl.BlockSpec((tc, d), lambda i: (i, 0)),
            pl.BlockSpec((tc, TOP_K), lambda i: (i, 0)),
            pl.BlockSpec((1, N_MOD, d), lambda i: (mod_row(i * tc), 0, 0)),
            _resident((1, d), lambda i: (0, 0)),
            pl.BlockSpec(memory_space=pl.ANY),
        ],
        out_specs=pl.BlockSpec((tc, d), lambda i: (i, 0)),
        scratch_shapes=[pltpu.VMEM((2, TOP_K, tc, d), F32), pltpu.SemaphoreType.DMA((2,))],
        compiler_params=_cparams(("arbitrary",)),
        name="combine",
    )(dest3, dest3, x1, gate_w, mod3, final_g.reshape(1, d), y)


def _rope_tables(seq):
    rows = seq // GRID_W
    row = jnp.repeat(jnp.arange(rows), GRID_W).astype(F32)
    col = jnp.tile(jnp.arange(GRID_W), rows).astype(F32)
    freqs = ROPE_THETA ** (-jnp.arange(ROPE_PAIRS_PER_AXIS, dtype=F32) / ROPE_PAIRS_PER_AXIS)
    ang = jnp.concatenate([row[:, None] * freqs, col[:, None] * freqs], axis=-1)
    cos, sin = jnp.cos(ang), jnp.sin(ang)
    return jnp.concatenate([cos, cos], axis=-1), jnp.concatenate([-sin, sin], axis=-1)


def _slot_tiles(n_tokens, n_experts):
    return (n_tokens * TOP_K) // EXPERT_TILE_ROWS + n_experts


def _moe(groups, wts, final_g, mod3, xs0):
    te = EXPERT_TILE_ROWS
    sizes = [g[0].shape[0] for g in groups]
    n = sum(sizes)
    logits = jnp.concatenate([g[2] for g in groups], axis=0)
    ne = logits.shape[1]
    top_i, gate_w, rank, counts = _routing(logits)

    counts = counts.reshape(ne).astype(I32)
    padded = (counts + te - 1) // te * te
    pad_end = jnp.cumsum(padded)
    pad_start = pad_end - padded
    n_tiles = _slot_tiles(n, ne)
    assert xs0.shape[0] == n_tiles * te
    tile_start = jnp.arange(n_tiles, dtype=I32) * te
    tile_expert = jnp.minimum(jnp.sum((pad_end[None, :] <= tile_start[:, None]).astype(I32), axis=1), ne - 1)
    tile_rows = jnp.clip(counts[tile_expert] - (tile_start - pad_start[tile_expert]), 0, te).astype(I32)
    n_used = (pad_end[-1:] // te).astype(I32)
    dest = pad_start[top_i] + rank

    offs = [sum(sizes[:i]) for i in range(len(sizes))]
    xs = xs0
    for (_, h2, _, _), o, sz in zip(groups, offs, sizes):
        xs = _dispatch(h2, dest[o:o + sz], xs)
    y = _expert_mlp(xs, tile_expert, tile_rows, n_used, wts["w_gu"], wts["b_gu"], wts["w_down"], wts["b_down"], te)
    return [_combine(x1, gate_w[o:o + sz], dest[o:o + sz], y, mod3, mod_row, final_g)
            for (x1, _, _, mod_row), o, sz in zip(groups, offs, sizes)]


def _mixer(x2d, batch, seq, mod3, mod_row, rope, kv_ctx, wts, emit_kv_f32, cast_jobs=(), zero_jobs=()):
    outs = _in_projection(x2d, seq, mod3, mod_row, wts["norm1_g"], wts["w_in"], wts["q_norm_g"],
                          wts["k_norm_g"], rope, emit_kv_f32)
    h, u, q, k, v = outs[:5]
    d_q = q.shape[1]
    d_kv = k.shape[1]
    q3 = q.reshape(batch, seq, d_q)
    k3 = k.reshape(batch, seq, d_kv)
    v3 = v.reshape(batch, seq, d_kv)
    if kv_ctx is not None:
        k3 = jnp.concatenate([kv_ctx[0], k3], axis=1)
        v3 = jnp.concatenate([kv_ctx[1], v3], axis=1)
    attn, casts, zeros = _attention(q3, k3, v3, cast_jobs, zero_jobs)
    attn = attn.reshape(batch * seq, d_q)
    pm = _pool_mixer(u, seq, wts["w_pool"], wts["pool_scale"])
    merged = _gated_merge(h, pm, attn, wts["w_gate"], wts["b_gate"], wts["w_a_out"], wts["w_b_out"])
    x1, h2, logits = _out_projection(merged, x2d, mod3, mod_row, wts["w_o"], wts["norm2_g"],
                                     wts["w_router"], wts["b_router"])
    return (x1, h2, logits, mod_row), outs[5:], casts, zeros


def kernel(x_prompt, x_sample, cache_k, cache_v, c, c_ctx, w_mod, b_mod, norm1_g, w_in, q_norm_g, k_norm_g, w_pool, pool_scale, w_a_out, w_b_out, w_gate, b_gate, w_o, norm2_g, w_router, b_router, w_gu, b_gu, w_down, b_down, final_g):
    depth = w_mod.shape[0]
    assert depth == 1, "the final norm is fused into the last combine; one trunk layer is supported"
    bp, sp, d = x_prompt.shape
    bs, ss, _ = x_sample.shape
    l = 0

    n_rows = -(-(bs + 1) // SUBLANES) * SUBLANES
    cvec = jnp.zeros((n_rows, d), F32).at[:bs].set(c).at[bs].set(c_ctx)
    mod3 = _modulation(cvec, w_mod[l], b_mod[l]).reshape(n_rows, N_MOD, d)

    wts = dict(
        norm1_g=norm1_g[l], w_in=w_in[l].astype(BF16), q_norm_g=q_norm_g[l], k_norm_g=k_norm_g[l],
        w_pool=w_pool[l].astype(BF16), pool_scale=pool_scale[l], w_a_out=w_a_out[l].astype(BF16),
        w_b_out=w_b_out[l].astype(BF16), w_gate=w_gate[l].astype(BF16), b_gate=b_gate[l],
        w_o=w_o[l].astype(BF16), norm2_g=norm2_g[l], w_router=w_router[l].astype(BF16), b_router=b_router[l],
        b_gu=b_gu[l], b_down=b_down[l],
    )

    gp, (kf, vf), _, _ = _mixer(x_prompt.reshape(bp * sp, d), bp, sp, mod3, lambda t: bs, None, None,
                                wts, True)
    kv_ctx = (cache_k[:, l].reshape(bs, -1, N_KV_HEADS * HEAD_DIM).astype(BF16),
              cache_v[:, l].reshape(bs, -1, N_KV_HEADS * HEAD_DIM).astype(BF16))
    ne, _, d_ff2 = w_gu[l].shape
    n_slots = _slot_tiles(bp * sp + bs * ss, ne) * EXPERT_TILE_ROWS
    gs, _, (w_gu_b, w_down_b), (xs0,) = _mixer(
        x_sample.reshape(bs * ss, d), bs, ss, mod3, lambda t: t // ss, _rope_tables(ss), kv_ctx, wts, False,
        cast_jobs=(w_gu[l].reshape(ne * d, d_ff2), w_down[l].reshape(-1, d)), zero_jobs=((n_slots, d),))
    wts["w_gu"] = w_gu_b.reshape(ne, d, d_ff2)
    wts["w_down"] = w_down_b.reshape(w_down[l].shape)
    yp, ys = _moe([gp, gs], wts, final_g, mod3, xs0)

    y_prompt = yp.reshape(bp, sp, d)
    y_sample = ys.reshape(bs, ss, d)
    new_cache_k = kf.reshape(bp, 1, sp, N_KV_HEADS, HEAD_DIM)
    new_cache_v = vf.reshape(bp, 1, sp, N_KV_HEADS, HEAD_DIM)
    return (y_prompt, y_sample, new_cache_k, new_cache_v)
```

```python
import functools

import jax
import jax.numpy as jnp
from jax import lax
from jax.experimental import pallas as pl
from jax.experimental.pallas import tpu as pltpu

F32 = jnp.float32
BF16 = jnp.bfloat16
I32 = jnp.int32

GRID_W = 64
N_HEADS = 16
N_KV_HEADS = 4
HEAD_DIM = 128
Q_GROUP = N_HEADS // N_KV_HEADS
ROPE_PAIRS_PER_AXIS = HEAD_DIM // 4
ROPE_THETA = 10000.0
POOL_WINDOWS = (2, 4, 8, 16)
TOP_K = 4
SWIGLU_LIMIT = 7.0
SWIGLU_ALPHA = 1.702
N_MOD = 6
EPS = 1e-6

LANES = 128
SUBLANES = 8
BF16_SUBLANES = 16
VMEM_LIMIT_BYTES = 56 * 1024 * 1024
SIDE_BLOCK_BYTES = 6 * 1024 * 1024

EXPERT_TILE_ROWS = 512
POOL_HALO = SUBLANES


def _cparams(semantics):
    return pltpu.CompilerParams(dimension_semantics=semantics, vmem_limit_bytes=VMEM_LIMIT_BYTES)


def _pick(n, candidates):
    for c in candidates:
        if n % c == 0:
            return c
    raise ValueError(f"no tile in {candidates} divides {n}")


def _resident(shape, index_map):
    return pl.BlockSpec(shape, index_map, pipeline_mode=pl.Buffered(1))


def _mod_kernel(c_ref, w_ref, b_ref, o_ref):
    c = c_ref[...]
    s = (c * jax.nn.sigmoid(c)).astype(BF16)
    o_ref[...] = jnp.dot(s, w_ref[...].astype(BF16), preferred_element_type=F32) + b_ref[...]


def _modulation(cvec, w_mod, b_mod):
    r, d = cvec.shape
    n6 = w_mod.shape[1]
    tn = _pick(n6, (1024, 512, 256, 128))
    return pl.pallas_call(
        _mod_kernel,
        out_shape=jax.ShapeDtypeStruct((r, n6), F32),
        grid=(n6 // tn,),
        in_specs=[
            pl.BlockSpec((r, d), lambda j: (0, 0)),
            pl.BlockSpec((d, tn), lambda j: (0, j)),
            pl.BlockSpec((1, tn), lambda j: (0, j)),
        ],
        out_specs=pl.BlockSpec((r, tn), lambda j: (0, j)),
        compiler_params=_cparams(("arbitrary",)),
        name="modulation",
    )(cvec, w_mod, b_mod.reshape(1, n6))


def _inproj_kernel(*refs, use_rope, emit_kv_f32, d_pool):
    it = iter(refs)
    x_ref, mod_ref, g1_ref, w_ref, qg_ref, kg_ref = (next(it) for _ in range(6))
    cos_ref = sin_ref = None
    if use_rope:
        cos_ref, sin_ref = next(it), next(it)
    h_ref, u_ref, q_ref, k_ref, v_ref = (next(it) for _ in range(5))
    kf_ref = vf_ref = None
    if emit_kv_f32:
        kf_ref, vf_ref = next(it), next(it)

    x = x_ref[...]
    ms = jnp.mean(x * x, axis=-1, keepdims=True)
    xn = x * lax.rsqrt(ms + EPS) * g1_ref[...]
    shift = mod_ref[0, 0:1, :]
    scale = mod_ref[0, 1:2, :]
    hb = (xn * (1.0 + scale) + shift).astype(BF16)
    h_ref[...] = hb

    d_q = N_HEADS * HEAD_DIM
    d_kv = N_KV_HEADS * HEAD_DIM
    chunk = Q_GROUP * HEAD_DIM

    for c0 in range(0, d_pool, chunk):
        u_ref[:, c0:c0 + chunk] = jnp.dot(hb, w_ref[:, c0:c0 + chunk], preferred_element_type=F32)

    if use_rope:
        cos = cos_ref[...]
        sin = sin_ref[...]

    def head(xh, g):
        hm = jnp.mean(xh * xh, axis=-1, keepdims=True)
        y = xh * lax.rsqrt(hm + EPS) * g
        if use_rope:
            y = y * cos + pltpu.roll(y, HEAD_DIM // 2, axis=1) * sin
        return y

    qg = qg_ref[...]
    kg = kg_ref[...]
    for c0 in range(0, d_q, chunk):
        pr = jnp.dot(hb, w_ref[:, d_pool + c0:d_pool + c0 + chunk], preferred_element_type=F32)
        for j in range(chunk // HEAD_DIM):
            yh = head(pr[:, j * HEAD_DIM:(j + 1) * HEAD_DIM], qg)
            q_ref[:, c0 + j * HEAD_DIM:c0 + (j + 1) * HEAD_DIM] = yh.astype(BF16)

    pk = jnp.dot(hb, w_ref[:, d_pool + d_q:d_pool + d_q + d_kv], preferred_element_type=F32)
    for j in range(N_KV_HEADS):
        yh = head(pk[:, j * HEAD_DIM:(j + 1) * HEAD_DIM], kg)
        k_ref[:, j * HEAD_DIM:(j + 1) * HEAD_DIM] = yh.astype(BF16)
        if emit_kv_f32:
            kf_ref[:, j * HEAD_DIM:(j + 1) * HEAD_DIM] = yh

    pv = jnp.dot(hb, w_ref[:, d_pool + d_q + d_kv:], preferred_element_type=F32)
    v_ref[...] = pv.astype(BF16)
    if emit_kv_f32:
        vf_ref[...] = pv


def _in_projection(x2d, seq, mod3, mod_row, g1, w_in_b, qg, kg, rope, emit_kv_f32):
    n, d = x2d.shape
    d_in = w_in_b.shape[1]
    d_q = N_HEADS * HEAD_DIM
    d_kv = N_KV_HEADS * HEAD_DIM
    d_pool = d_in - d_q - 2 * d_kv
    ts = _pick(seq, (512, 256, 128))
    tiles_per_seq = seq // ts
    use_rope = rope is not None

    in_specs = [
        pl.BlockSpec((ts, d), lambda i: (i, 0)),
        pl.BlockSpec((1, N_MOD, d), lambda i: (mod_row(i * ts), 0, 0)),
        _resident((1, d), lambda i: (0, 0)),
        _resident((d, d_in), lambda i: (0, 0)),
        _resident((1, HEAD_DIM), lambda i: (0, 0)),
        _resident((1, HEAD_DIM), lambda i: (0, 0)),
    ]
    args = [x2d, mod3, g1.reshape(1, d), w_in_b, qg.reshape(1, HEAD_DIM), kg.reshape(1, HEAD_DIM)]
    if use_rope:
        in_specs += [pl.BlockSpec((ts, HEAD_DIM), lambda i: (i % tiles_per_seq, 0))] * 2
        args += list(rope)

    out_shape = [
        jax.ShapeDtypeStruct((n, d), BF16),
        jax.ShapeDtypeStruct((n, d_pool), F32),
        jax.ShapeDtypeStruct((n, d_q), BF16),
        jax.ShapeDtypeStruct((n, d_kv), BF16),
        jax.ShapeDtypeStruct((n, d_kv), BF16),
    ]
    out_specs = [
        pl.BlockSpec((ts, d), lambda i: (i, 0)),
        pl.BlockSpec((ts, d_pool), lambda i: (i, 0)),
        pl.BlockSpec((ts, d_q), lambda i: (i, 0)),
        pl.BlockSpec((ts, d_kv), lambda i: (i, 0)),
        pl.BlockSpec((ts, d_kv), lambda i: (i, 0)),
    ]
    if emit_kv_f32:
        out_shape += [jax.ShapeDtypeStruct((n, d_kv), F32)] * 2
        out_specs += [pl.BlockSpec((ts, d_kv), lambda i: (i, 0))] * 2

    return pl.pallas_call(
        functools.partial(_inproj_kernel, use_rope=use_rope, emit_kv_f32=emit_kv_f32, d_pool=d_pool),
        out_shape=out_shape,
        grid=(n // ts,),
        in_specs=in_specs,
        out_specs=out_specs,
        compiler_params=_cparams(("arbitrary",)),
        name="in_projection",
    )(*args)


def _attn_kernel(*refs, tk, n_chunks, scale, n_cast, n_zero):
    q_ref, k_ref, v_ref = refs[:3]
    cast_in = refs[3:3 + n_cast]
    o_ref = refs[3 + n_cast]
    cast_out = refs[4 + n_cast:4 + 2 * n_cast]
    zero_out = refs[4 + 2 * n_cast:4 + 2 * n_cast + n_zero]
    for src, dst in zip(cast_in, cast_out):
        dst[...] = src[...].astype(dst.dtype)
    for dst in zero_out:
        dst[...] = jnp.zeros_like(dst)

    q = q_ref[0]
    tq = q.shape[0]
    qs = jnp.concatenate([q[:, j * HEAD_DIM:(j + 1) * HEAD_DIM] for j in range(Q_GROUP)], axis=0)
    rows = qs.shape[0]
    c2 = scale * 1.4426950408889634
    m = jnp.full((rows, 1), -jnp.inf, F32)
    l = jnp.zeros((rows, 1), F32)
    acc = jnp.zeros((rows, HEAD_DIM), F32)
    for c in range(n_chunks):
        kc = k_ref[0, c * tk:(c + 1) * tk, :]
        vc = v_ref[0, c * tk:(c + 1) * tk, :]
        s = lax.dot_general(qs, kc, (((1,), (1,)), ((), ())), preferred_element_type=F32) * c2
        m_new = jnp.maximum(m, jnp.max(s, axis=1, keepdims=True))
        alpha = jnp.exp2(m - m_new)
        p = jnp.exp2(s - m_new)
        l = alpha * l + jnp.sum(p, axis=1, keepdims=True)
        acc = alpha * acc + jnp.dot(p.astype(BF16), vc, preferred_element_type=F32)
        m = m_new
    out = acc / l
    o_ref[0] = jnp.concatenate([out[j * tq:(j + 1) * tq] for j in range(Q_GROUP)], axis=1).astype(o_ref.dtype)


def _side_rows(total_rows, row_bytes, n_steps):
    if total_rows % n_steps:
        return None
    rows = total_rows // n_steps
    if rows % BF16_SUBLANES or rows * row_bytes > SIDE_BLOCK_BYTES:
        return None
    return rows


def _attention(q3, k3, v3, cast_jobs=(), zero_jobs=()):
    b, s, d_q = q3.shape
    kv_len = k3.shape[1]
    tq = _pick(s, (512, 256, 128))
    tk = _pick(kv_len, (256, 128))
    gw = Q_GROUP * HEAD_DIM
    nq = s // tq
    n_steps = b * N_KV_HEADS * nq

    def step(bi, g, qi):
        return ((bi * N_KV_HEADS + g) * nq + qi, 0)

    cast_rows = [_side_rows(a.shape[0], a.shape[1] * 4, n_steps) for a in cast_jobs]
    zero_rows = [_side_rows(r, c * 4, n_steps) for r, c in zero_jobs]
    fused_cast = [a for a, r in zip(cast_jobs, cast_rows) if r is not None]
    fused_zero = [z for z, r in zip(zero_jobs, zero_rows) if r is not None]

    in_specs = [
        pl.BlockSpec((1, tq, gw), lambda bi, g, qi: (bi, qi, g)),
        pl.BlockSpec((1, kv_len, HEAD_DIM), lambda bi, g, qi: (bi, 0, g)),
        pl.BlockSpec((1, kv_len, HEAD_DIM), lambda bi, g, qi: (bi, 0, g)),
    ]
    out_shape = [jax.ShapeDtypeStruct((b, s, d_q), BF16)]
    out_specs = [pl.BlockSpec((1, tq, gw), lambda bi, g, qi: (bi, qi, g))]
    for a, r in zip(cast_jobs, cast_rows):
        if r is not None:
            in_specs.append(pl.BlockSpec((r, a.shape[1]), step))
            out_shape.append(jax.ShapeDtypeStruct(a.shape, BF16))
            out_specs.append(pl.BlockSpec((r, a.shape[1]), step))
    for (rows, cols), r in zip(zero_jobs, zero_rows):
        if r is not None:
            out_shape.append(jax.ShapeDtypeStruct((rows, cols), F32))
            out_specs.append(pl.BlockSpec((r, cols), step))

    outs = pl.pallas_call(
        functools.partial(_attn_kernel, tk=tk, n_chunks=kv_len // tk, scale=HEAD_DIM ** -0.5,
                          n_cast=len(fused_cast), n_zero=len(fused_zero)),
        out_shape=out_shape,
        grid=(b, N_KV_HEADS, nq),
        in_specs=in_specs,
        out_specs=out_specs,
        compiler_params=_cparams(("arbitrary", "arbitrary", "arbitrary")),
        name="attention",
    )(q3, k3, v3, *fused_cast)

    it = iter(outs[1:])
    casts = [next(it) if r is not None else a.astype(BF16) for a, r in zip(cast_jobs, cast_rows)]
    zeros = [next(it) if r is not None else jnp.zeros(z, F32) for z, r in zip(zero_jobs, zero_rows)]
    return outs[0], casts, zeros


def _pool_kernel(up_ref, u_ref, un_ref, wp_ref, ps_ref, o_ref, ext_ref, *, tp, seq):
    i = pl.program_id(0)
    pos0 = (i * tp) % seq
    has_prev = pos0 > 0
    has_next = pos0 + tp < seq
    h = POOL_HALO
    ext_ref[0:h, :] = jnp.where(has_prev, up_ref[...], 0.0)
    ext_ref[h:h + tp, :] = u_ref[...]
    ext_ref[h + tp:h + tp + h, :] = jnp.where(has_next, un_ref[...], 0.0)

    gdim = wp_ref.shape[1]
    pos = pos0 + lax.broadcasted_iota(I32, (tp, 1), 0)
    for g, w in enumerate(POOL_WINDOWS):
        c0 = g * gdim
        acc = jnp.zeros((tp, gdim), F32)
        for j in range(-(w // 2), w - w // 2):
            acc = acc + ext_ref[h + j:h + j + tp, c0:c0 + gdim]
        lo = jnp.maximum(pos - w // 2, 0)
        hi = jnp.minimum(pos + (w - w // 2), seq)
        mean = acc / (hi - lo).astype(F32)
        pooled = (mean - u_ref[:, c0:c0 + gdim]).astype(BF16)
        y = jnp.dot(pooled, wp_ref[g], preferred_element_type=F32) * ps_ref[:, c0:c0 + gdim]
        o_ref[:, c0:c0 + gdim] = y.astype(o_ref.dtype)


def _pool_mixer(u2d, seq, w_pool_b, pool_scale):
    n, d_pool = u2d.shape
    tp = _pick(seq, (512, 256, 128))
    h = POOL_HALO
    assert max(POOL_WINDOWS) // 2 <= h
    nb = n // h
    per = tp // h
    return pl.pallas_call(
        functools.partial(_pool_kernel, tp=tp, seq=seq),
        out_shape=jax.ShapeDtypeStruct((n, d_pool), BF16),
        grid=(n // tp,),
        in_specs=[
            pl.BlockSpec((h, d_pool), lambda i: (jnp.maximum(i * per - 1, 0), 0)),
            pl.BlockSpec((tp, d_pool), lambda i: (i, 0)),
            pl.BlockSpec((h, d_pool), lambda i: (jnp.minimum((i + 1) * per, nb - 1), 0)),
            _resident(w_pool_b.shape, lambda i: (0, 0, 0)),
            _resident((1, d_pool), lambda i: (0, 0)),
        ],
        out_specs=pl.BlockSpec((tp, d_pool), lambda i: (i, 0)),
        scratch_shapes=[pltpu.VMEM((tp + 2 * h, d_pool), F32)],
        compiler_params=_cparams(("arbitrary",)),
        name="pool_mixer",
    )(u2d, u2d, u2d, w_pool_b, pool_scale.reshape(1, d_pool))


def _merge_kernel(h_ref, pm_ref, at_ref, wga_ref, wgb_ref, bga_ref, bgb_ref, wa_ref, wb_ref, o_ref):
    hb = h_ref[...]
    ga = jax.nn.sigmoid(jnp.dot(hb, wga_ref[...], preferred_element_type=F32) + bga_ref[...])
    gb = jax.nn.sigmoid(jnp.dot(hb, wgb_ref[...], preferred_element_type=F32) + bgb_ref[...])
    ya = jnp.dot(pm_ref[...], wa_ref[...], preferred_element_type=F32)
    yb = jnp.dot(at_ref[...], wb_ref[...], preferred_element_type=F32)
    o_ref[...] = (ga * ya + gb * yb).astype(o_ref.dtype)


def _gated_merge(h2d, pm2d, at2d, w_gate_b, b_gate, w_a_b, w_b_b):
    n, d = h2d.shape
    d_pool = pm2d.shape[1]
    d_q = at2d.shape[1]
    tm = _pick(n, (1024, 512, 256, 128))
    tn = _pick(d, (512, 256, 128))
    nj = d // tn
    bg = b_gate.reshape(1, 2 * d)
    return pl.pallas_call(
        _merge_kernel,
        out_shape=jax.ShapeDtypeStruct((n, d), BF16),
        grid=(n // tm, nj),
        in_specs=[
            pl.BlockSpec((tm, d), lambda i, j: (i, 0)),
            pl.BlockSpec((tm, d_pool), lambda i, j: (i, 0)),
            pl.BlockSpec((tm, d_q), lambda i, j: (i, 0)),
            pl.BlockSpec((d, tn), lambda i, j: (0, j)),
            pl.BlockSpec((d, tn), lambda i, j: (0, nj + j)),
            pl.BlockSpec((1, tn), lambda i, j: (0, j)),
            pl.BlockSpec((1, tn), lambda i, j: (0, nj + j)),
            pl.BlockSpec((d_pool, tn), lambda i, j: (0, j)),
            pl.BlockSpec((d_q, tn), lambda i, j: (0, j)),
        ],
        out_specs=pl.BlockSpec((tm, tn), lambda i, j: (i, j)),
        compiler_params=_cparams(("arbitrary", "arbitrary")),
        name="gated_merge",
    )(h2d, pm2d, at2d, w_gate_b, w_gate_b, bg, bg, w_a_b, w_b_b)


def _outproj_kernel(mg_ref, x_ref, mod_ref, wo_ref, g2_ref, wr_ref, br_ref, x1_ref, h2_ref, lg_ref):
    mix = jnp.dot(mg_ref[...], wo_ref[...], preferred_element_type=F32)
    gate1 = mod_ref[0, 2:3, :]
    shift2 = mod_ref[0, 3:4, :]
    scale2 = mod_ref[0, 4:5, :]
    x1 = x_ref[...] + gate1 * mix
    x1_ref[...] = x1
    ms = jnp.mean(x1 * x1, axis=-1, keepdims=True)
    h2 = (x1 * lax.rsqrt(ms + EPS) * g2_ref[...]) * (1.0 + scale2) + shift2
    h2_ref[...] = h2
    lg_ref[...] = jnp.dot(h2.astype(BF16), wr_ref[...], preferred_element_type=F32) + br_ref[...]


def _out_projection(mg2d, x2d, mod3, mod_row, w_o_b, g2, w_router, b_router):
    n, d = x2d.shape
    ne = w_router.shape[1]
    tm = _pick(n, (256, 128))
    return pl.pallas_call(
        _outproj_kernel,
        out_shape=[
            jax.ShapeDtypeStruct((n, d), F32),
            jax.ShapeDtypeStruct((n, d), F32),
            jax.ShapeDtypeStruct((n, ne), F32),
        ],
        grid=(n // tm,),
        in_specs=[
            pl.BlockSpec((tm, d), lambda i: (i, 0)),
            pl.BlockSpec((tm, d), lambda i: (i, 0)),
            pl.BlockSpec((1, N_MOD, d), lambda i: (mod_row(i * tm), 0, 0)),
            _resident((d, d), lambda i: (0, 0)),
            _resident((1, d), lambda i: (0, 0)),
            _resident((d, ne), lambda i: (0, 0)),
            _resident((1, ne), lambda i: (0, 0)),
        ],
        out_specs=[
            pl.BlockSpec((tm, d), lambda i: (i, 0)),
            pl.BlockSpec((tm, d), lambda i: (i, 0)),
            pl.BlockSpec((tm, ne), lambda i: (i, 0)),
        ],
        compiler_params=_cparams(("arbitrary",)),
        name="out_projection",
    )(mg2d, x2d, mod3, w_o_b, g2.reshape(1, d), w_router, b_router.reshape(1, ne))


def _route_kernel(lg_ref, idx_ref, gw_ref, rk_ref, cnt_ref):
    step = pl.program_id(0)

    @pl.when(step == 0)
    def _():
        cnt_ref[...] = jnp.zeros_like(cnt_ref)

    lg = lg_ref[...]
    tr, ne = lg.shape
    lane = lax.broadcasted_iota(I32, (tr, ne), 1)
    work = lg
    vals, idxs = [], []
    member = jnp.zeros((tr, ne), F32)
    for _ in range(TOP_K):
        mx = jnp.max(work, axis=-1, keepdims=True)
        ix = jnp.min(jnp.where(work == mx, lane, ne), axis=-1, keepdims=True)
        sel = lane == ix
        vals.append(mx)
        idxs.append(ix)
        member = jnp.where(sel, 1.0, member)
        work = jnp.where(sel, -jnp.inf, work)

    ex = [jnp.exp(v - vals[0]) for v in vals]
    den = ex[0]
    for e in ex[1:]:
        den = den + e

    row = lax.broadcasted_iota(I32, (tr, tr), 0)
    col = lax.broadcasted_iota(I32, (tr, tr), 1)
    lower = jnp.where(col < row, 1.0, 0.0).astype(BF16)
    before = jnp.dot(lower, member.astype(BF16), preferred_element_type=F32) + cnt_ref[...]

    k_lane = lax.broadcasted_iota(I32, (tr, TOP_K), 1)
    idx_out = jnp.zeros((tr, TOP_K), I32)
    gw_out = jnp.zeros((tr, TOP_K), F32)
    rk_out = jnp.zeros((tr, TOP_K), I32)
    for k in range(TOP_K):
        rk = jnp.sum(jnp.where(lane == idxs[k], before, 0.0), axis=-1, keepdims=True).astype(I32)
        idx_out = jnp.where(k_lane == k, idxs[k], idx_out)
        gw_out = jnp.where(k_lane == k, ex[k] / den, gw_out)
        rk_out = jnp.where(k_lane == k, rk, rk_out)
    idx_ref[...] = idx_out
    gw_ref[...] = gw_out
    rk_ref[...] = rk_out
    cnt_ref[...] += jnp.sum(member, axis=0, keepdims=True)


def _routing(logits):
    n, ne = logits.shape
    tr = _pick(n, (512, 256, 128))
    return pl.pallas_call(
        _route_kernel,
        out_shape=[
            jax.ShapeDtypeStruct((n, TOP_K), I32),
            jax.ShapeDtypeStruct((n, TOP_K), F32),
            jax.ShapeDtypeStruct((n, TOP_K), I32),
            jax.ShapeDtypeStruct((1, ne), F32),
        ],
        grid=(n // tr,),
        in_specs=[pl.BlockSpec((tr, ne), lambda i: (i, 0))],
        out_specs=[
            pl.BlockSpec((tr, TOP_K), lambda i: (i, 0)),
            pl.BlockSpec((tr, TOP_K), lambda i: (i, 0)),
            pl.BlockSpec((tr, TOP_K), lambda i: (i, 0)),
            pl.BlockSpec((1, ne), lambda i: (0, 0)),
        ],
        compiler_params=_cparams(("arbitrary",)),
        name="routing",
    )(logits)


def _dispatch_kernel(dest_ref, h_ref, xs_in_hbm, xs_hbm, sem, *, td):
    del xs_in_hbm

    def start(r, carry):
        for k in range(TOP_K):
            d = dest_ref[0, 0, r * TOP_K + k]
            pltpu.make_async_copy(h_ref.at[pl.ds(r, 1)], xs_hbm.at[pl.ds(d, 1)], sem).start()
        return carry

    lax.fori_loop(0, td, start, 0)
    for _ in range(TOP_K):
        pltpu.make_async_copy(h_ref, xs_hbm.at[pl.ds(0, td)], sem).wait()


def _dispatch(h2d, dest, xs0):
    n, d = h2d.shape
    n_slots = xs0.shape[0]
    td = _pick(n, (512, 256, 128))
    dest3 = dest.reshape(n // td, 1, td * TOP_K)
    return pl.pallas_call(
        functools.partial(_dispatch_kernel, td=td),
        out_shape=jax.ShapeDtypeStruct((n_slots, d), h2d.dtype),
        grid=(n // td,),
        in_specs=[
            pl.BlockSpec((1, 1, td * TOP_K), lambda i: (i, 0, 0), memory_space=pltpu.SMEM),
            pl.BlockSpec((td, d), lambda i: (i, 0)),
            pl.BlockSpec(memory_space=pl.ANY),
        ],
        out_specs=pl.BlockSpec(memory_space=pl.ANY),
        scratch_shapes=[pltpu.SemaphoreType.DMA(())],
        input_output_aliases={2: 0},
        compiler_params=_cparams(("arbitrary",)),
        name="dispatch",
    )(dest3, h2d, xs0)


def _expert_kernel(te_ref, tr_ref, nu_ref, x_ref, wg_ref, wl_ref, bg_ref, bl_ref, wd_ref, bd_ref, y_ref, *, nf):
    del te_ref
    i = pl.program_id(0)
    f = pl.program_id(1)
    used = i < nu_ref[0]
    rows = tr_ref[jnp.minimum(i, nu_ref[0] - 1)]
    te = x_ref.shape[0]
    half = te // 2

    def mlp(m, first):
        xb = x_ref[0:m, :].astype(BF16)
        g = jnp.dot(xb, wg_ref[0], preferred_element_type=F32) + bg_ref[0]
        lin = jnp.dot(xb, wl_ref[0], preferred_element_type=F32) + bl_ref[0]
        g = jnp.minimum(g, SWIGLU_LIMIT)
        lin = jnp.clip(lin, -SWIGLU_LIMIT, SWIGLU_LIMIT)
        act = g * jax.nn.sigmoid(SWIGLU_ALPHA * g) * (lin + 1.0)
        down = jnp.dot(act.astype(BF16), wd_ref[0], preferred_element_type=F32)
        if first:
            y_ref[0:m, :] = down + bd_ref[0]
            if m < te:
                y_ref[m:te, :] = jnp.zeros((te - m, y_ref.shape[1]), F32)
        else:
            y_ref[0:m, :] += down

    for m, occupied in ((te, rows > half), (half, rows <= half)):
        for first in (True, False):
            at_step = (f == 0) if first else (f != 0)

            @pl.when(jnp.logical_and(jnp.logical_and(used, occupied), at_step))
            def _():
                mlp(m, first)

    @pl.when(jnp.logical_and(jnp.logical_not(used), f == nf - 1))
    def _():
        y_ref[...] = jnp.zeros_like(y_ref)


def _expert_mlp(xs, tile_expert, tile_rows, n_used, w_gu_b, b_gu, w_down_b, b_down, te):
    n_slots, d = xs.shape
    ne, _, d_ff2 = w_gu_b.shape
    d_ff = d_ff2 // 2
    tf = _pick(d_ff, (1024, 512, 256, 128))
    nf = d_ff // tf
    n_tiles = n_slots // te

    def tile(i, nu):
        return jnp.minimum(i, nu[0] - 1)

    def ff(i, f, nu):
        return jnp.where(i < nu[0], f, nf - 1)

    grid_spec = pltpu.PrefetchScalarGridSpec(
        num_scalar_prefetch=3,
        grid=(n_tiles, nf),
        in_specs=[
            pl.BlockSpec((te, d), lambda i, f, tx, tr, nu: (tile(i, nu), 0)),
            pl.BlockSpec((1, d, tf), lambda i, f, tx, tr, nu: (tx[tile(i, nu)], 0, ff(i, f, nu))),
            pl.BlockSpec((1, d, tf), lambda i, f, tx, tr, nu: (tx[tile(i, nu)], 0, nf + ff(i, f, nu))),
            pl.BlockSpec((1, 1, tf), lambda i, f, tx, tr, nu: (tx[tile(i, nu)], 0, ff(i, f, nu))),
            pl.BlockSpec((1, 1, tf), lambda i, f, tx, tr, nu: (tx[tile(i, nu)], 0, nf + ff(i, f, nu))),
            pl.BlockSpec((1, tf, d), lambda i, f, tx, tr, nu: (tx[tile(i, nu)], ff(i, f, nu), 0)),
            pl.BlockSpec((1, 1, d), lambda i, f, tx, tr, nu: (tx[tile(i, nu)], 0, 0)),
        ],
        out_specs=pl.BlockSpec((te, d), lambda i, f, tx, tr, nu: (i, 0)),
    )
    return pl.pallas_call(
        functools.partial(_expert_kernel, nf=nf),
        out_shape=jax.ShapeDtypeStruct((n_slots, d), F32),
        grid_spec=grid_spec,
        compiler_params=_cparams(("arbitrary", "arbitrary")),
        name="expert_mlp",
    )(tile_expert, tile_rows, n_used, xs, w_gu_b, w_gu_b, b_gu.reshape(ne, 1, d_ff2), b_gu.reshape(ne, 1, d_ff2),
      w_down_b, b_down.reshape(ne, 1, d))


def _combine_kernel(dcur_ref, dnext_ref, x1_ref, gw_ref, mod_ref, gf_ref, y_hbm, o_ref, ybuf, sems, *,
                    tc, n_steps):
    i = pl.program_id(0)
    slot = i % 2

    def issue(dest_ref, s):
        def start(r, carry):
            for k in range(TOP_K):
                d = dest_ref[0, 0, r * TOP_K + k]
                pltpu.make_async_copy(y_hbm.at[pl.ds(d, 1)], ybuf.at[s, k, pl.ds(r, 1)], sems.at[s]).start()
            return carry

        lax.fori_loop(0, tc, start, 0)

    @pl.when(i == 0)
    def _():
        issue(dcur_ref, 0)

    for s in range(2):
        @pl.when(jnp.logical_and(i + 1 < n_steps, slot == 1 - s))
        def _():
            issue(dnext_ref, s)

    for k in range(TOP_K):
        pltpu.make_async_copy(y_hbm.at[pl.ds(0, tc)], ybuf.at[slot, k], sems.at[slot]).wait()

    gw = gw_ref[...]
    moe = gw[:, 0:1] * ybuf[slot, 0]
    for k in range(1, TOP_K):
        moe = moe + gw[:, k:k + 1] * ybuf[slot, k]
    gate2 = mod_ref[0, 5:6, :]
    x2 = x1_ref[...] + gate2 * moe
    ms = jnp.mean(x2 * x2, axis=-1, keepdims=True)
    o_ref[...] = x2 * lax.rsqrt(ms + EPS) * gf_ref[...]


def _combine(x1, gate_w, dest, y, mod3, mod_row, final_g):
    n, d = x1.shape
    tc = _pick(n, (256, 128))
    n_steps = n // tc
    dest3 = dest.reshape(n_steps, 1, tc * TOP_K)
    return pl.pallas_call(
        functools.partial(_combine_kernel, tc=tc, n_steps=n_steps),
        out_shape=jax.ShapeDtypeStruct((n, d), F32),
        grid=(n_steps,),
        in_specs=[
            pl.BlockSpec((1, 1, tc * TOP_K), lambda i: (i, 0, 0), memory_space=pltpu.SMEM),
            pl.BlockSpec((1, 1, tc * TOP_K), lambda i: (jnp.minimum(i + 1, n_steps - 1), 0, 0),
                         memory_space=pltpu.SMEM),
            pl.BlockSpec((tc, d), lambda i: (i, 0)),
            pl.BlockSpec((tc, TOP_K), lambda i: (i, 0)),
            pl.BlockSpec((1, N_MOD, d), lambda i: (mod_row(i * tc), 0, 0)),
            _resident((1, d), lambda i: (0, 0)),
            pl.BlockSpec(memory_space=pl.ANY),
        ],
        out_specs=pl.BlockSpec((tc, d), lambda i: (i, 0)),
        scratch_shapes=[pltpu.VMEM((2, TOP_K, tc, d), F32), pltpu.SemaphoreType.DMA((2,))],
        compiler_params=_cparams(("arbitrary",)),
        name="combine",
    )(dest3, dest3, x1, gate_w, mod3, final_g.reshape(1, d), y)


def _rope_tables(seq):
    rows = seq // GRID_W
    row = jnp.repeat(jnp.arange(rows), GRID_W).astype(F32)
    col = jnp.tile(jnp.arange(GRID_W), rows).astype(F32)
    freqs = ROPE_THETA ** (-jnp.arange(ROPE_PAIRS_PER_AXIS, dtype=F32) / ROPE_PAIRS_PER_AXIS)
    ang = jnp.concatenate([row[:, None] * freqs, col[:, None] * freqs], axis=-1)
    cos, sin = jnp.cos(ang), jnp.sin(ang)
    return jnp.concatenate([cos, cos], axis=-1), jnp.concatenate([-sin, sin], axis=-1)


def _slot_tiles(n_tokens, n_experts):
    return (n_tokens * TOP_K) // EXPERT_TILE_ROWS + n_experts


def _moe(groups, wts, final_g, mod3, xs0):
    te = EXPERT_TILE_ROWS
    sizes = [g[0].shape[0] for g in groups]
    n = sum(sizes)
    logits = jnp.concatenate([g[2] for g in groups], axis=0)
    ne = logits.shape[1]
    top_i, gate_w, rank, counts = _routing(logits)

    counts = counts.reshape(ne).astype(I32)
    padded = (counts + te - 1) // te * te
    pad_end = jnp.cumsum(padded)
    pad_start = pad_end - padded
    n_tiles = _slot_tiles(n, ne)
    assert xs0.shape[0] == n_tiles * te
    tile_start = jnp.arange(n_tiles, dtype=I32) * te
    tile_expert = jnp.minimum(jnp.sum((pad_end[None, :] <= tile_start[:, None]).astype(I32), axis=1), ne - 1)
    tile_rows = jnp.clip(counts[tile_expert] - (tile_start - pad_start[tile_expert]), 0, te).astype(I32)
    n_used = (pad_end[-1:] // te).astype(I32)
    dest = pad_start[top_i] + rank

    offs = [sum(sizes[:i]) for i in range(len(sizes))]
    xs = xs0
    for (_, h2, _, _), o, sz in zip(groups, offs, sizes):
        xs = _dispatch(h2, dest[o:o + sz], xs)
    y = _expert_mlp(xs, tile_expert, tile_rows, n_used, wts["w_gu"], wts["b_gu"], wts["w_down"], wts["b_down"], te)
    return [_combine(x1, gate_w[o:o + sz], dest[o:o + sz], y, mod3, mod_row, final_g)
            for (x1, _, _, mod_row), o, sz in zip(groups, offs, sizes)]


def _mixer(x2d, batch, seq, mod3, mod_row, rope, kv_ctx, wts, emit_kv_f32, cast_jobs=(), zero_jobs=()):
    outs = _in_projection(x2d, seq, mod3, mod_row, wts["norm1_g"], wts["w_in"], wts["q_norm_g"],
                          wts["k_norm_g"], rope, emit_kv_f32)
    h, u, q, k, v = outs[:5]
    d_q = q.shape[1]
    d_kv = k.shape[1]
    q3 = q.reshape(batch, seq, d_q)
    k3 = k.reshape(batch, seq, d_kv)
    v3 = v.reshape(batch, seq, d_kv)
    if kv_ctx is not None:
        k3 = jnp.concatenate([kv_ctx[0], k3], axis=1)
        v3 = jnp.concatenate([kv_ctx[1], v3], axis=1)
    attn, casts, zeros = _attention(q3, k3, v3, cast_jobs, zero_jobs)
    attn = attn.reshape(batch * seq, d_q)
    pm = _pool_mixer(u, seq, wts["w_pool"], wts["pool_scale"])
    merged = _gated_merge(h, pm, attn, wts["w_gate"], wts["b_gate"], wts["w_a_out"], wts["w_b_out"])
    x1, h2, logits = _out_projection(merged, x2d, mod3, mod_row, wts["w_o"], wts["norm2_g"],
                                     wts["w_router"], wts["b_router"])
    return (x1, h2, logits, mod_row), outs[5:], casts, zeros


def kernel(x_prompt, x_sample, cache_k, cache_v, c, c_ctx, w_mod, b_mod, norm1_g, w_in, q_norm_g, k_norm_g, w_pool, pool_scale, w_a_out, w_b_out, w_gate, b_gate, w_o, norm2_g, w_router, b_router, w_gu, b_gu, w_down, b_down, final_g):
    depth = w_mod.shape[0]
    assert depth == 1, "the final norm is fused into the last combine; one trunk layer is supported"
    bp, sp, d = x_prompt.shape
    bs, ss, _ = x_sample.shape
    l = 0

    n_rows = -(-(bs + 1) // SUBLANES) * SUBLANES
    cvec = jnp.zeros((n_rows, d), F32).at[:bs].set(c).at[bs].set(c_ctx)
    mod3 = _modulation(cvec, w_mod[l], b_mod[l]).reshape(n_rows, N_MOD, d)

    wts = dict(
        norm1_g=norm1_g[l], w_in=w_in[l].astype(BF16), q_norm_g=q_norm_g[l], k_norm_g=k_norm_g[l],
        w_pool=w_pool[l].astype(BF16), pool_scale=pool_scale[l], w_a_out=w_a_out[l].astype(BF16),
        w_b_out=w_b_out[l].astype(BF16), w_gate=w_gate[l].astype(BF16), b_gate=b_gate[l],
        w_o=w_o[l].astype(BF16), norm2_g=norm2_g[l], w_router=w_router[l].astype(BF16), b_router=b_router[l],
        b_gu=b_gu[l], b_down=b_down[l],
    )

    gp, (kf, vf), _, _ = _mixer(x_prompt.reshape(bp * sp, d), bp, sp, mod3, lambda t: bs, None, None,
                                wts, True)
    kv_ctx = (cache_k[:, l].reshape(bs, -1, N_KV_HEADS * HEAD_DIM).astype(BF16),
              cache_v[:, l].reshape(bs, -1, N_KV_HEADS * HEAD_DIM).astype(BF16))
    ne, _, d_ff2 = w_gu[l].shape
    n_slots = _slot_tiles(bp * sp + bs * ss, ne) * EXPERT_TILE_ROWS
    gs, _, (w_gu_b, w_down_b), (xs0,) = _mixer(
        x_sample.reshape(bs * ss, d), bs, ss, mod3, lambda t: t // ss, _rope_tables(ss), kv_ctx, wts, False,
        cast_jobs=(w_gu[l].reshape(ne * d, d_ff2), w_down[l].reshape(-1, d)), zero_jobs=((n_slots, d),))
    wts["w_gu"] = w_gu_b.reshape(ne, d, d_ff2)
    wts["w_down"] = w_down_b.reshape(w_down[l].shape)
    yp, ys = _moe([gp, gs], wts, final_g, mod3, xs0)

    y_prompt = yp.reshape(bp, sp, d)
    y_sample = ys.reshape(bs, ss, d)
    new_cache_k = kf.reshape(bp, 1, sp, N_KV_HEADS, HEAD_DIM)
    new_cache_v = vf.reshape(bp, 1, sp, N_KV_HEADS, HEAD_DIM)
    return (y_prompt, y_sample, new_cache_k, new_cache_v)
```

```python
import functools
import math

import jax
import jax.numpy as jnp
from jax import lax
from jax.experimental import pallas as pl
from jax.experimental.pallas import tpu as pltpu

F32 = jnp.float32
BF16 = jnp.bfloat16
I32 = jnp.int32

GRID_W = 64
N_HEADS = 16
N_KV_HEADS = 4
HEAD_DIM = 128
Q_GROUP = N_HEADS // N_KV_HEADS
ROPE_PAIRS_PER_AXIS = HEAD_DIM // 4
ROPE_THETA = 10000.0
POOL_WINDOWS = (2, 4, 8, 16)
TOP_K = 4
SWIGLU_LIMIT = 7.0
SWIGLU_ALPHA = 1.702
N_MOD = 6
EPS = 1e-6

LANES = 128
SUBLANES = 8
BF16_SUBLANES = 16
VMEM_LIMIT_BYTES = 56 * 1024 * 1024
SIDE_BLOCK_BYTES = 6 * 1024 * 1024

EXPERT_TILE_ROWS = 512
POOL_HALO = SUBLANES


def _cparams(semantics):
    return pltpu.CompilerParams(dimension_semantics=semantics, vmem_limit_bytes=VMEM_LIMIT_BYTES)


def _pick(n, candidates):
    for c in candidates:
        if n % c == 0:
            return c
    raise ValueError(f"no tile in {candidates} divides {n}")


def _resident(shape, index_map):
    return pl.BlockSpec(shape, index_map, pipeline_mode=pl.Buffered(1))


def _mod_kernel(c_ref, w_ref, b_ref, o_ref):
    c = c_ref[...]
    s = (c * jax.nn.sigmoid(c)).astype(BF16)
    o_ref[...] = jnp.dot(s, w_ref[...].astype(BF16), preferred_element_type=F32) + b_ref[...]


def _modulation(cvec, w_mod, b_mod):
    r, d = cvec.shape
    n6 = w_mod.shape[1]
    tn = _pick(n6, (1024, 512, 256, 128))
    return pl.pallas_call(
        _mod_kernel,
        out_shape=jax.ShapeDtypeStruct((r, n6), F32),
        grid=(n6 // tn,),
        in_specs=[
            pl.BlockSpec((r, d), lambda j: (0, 0)),
            pl.BlockSpec((d, tn), lambda j: (0, j)),
            pl.BlockSpec((1, tn), lambda j: (0, j)),
        ],
        out_specs=pl.BlockSpec((r, tn), lambda j: (0, j)),
        compiler_params=_cparams(("arbitrary",)),
        name="modulation",
    )(cvec, w_mod, b_mod.reshape(1, n6))


def _inproj_kernel(*refs, use_rope, emit_kv_f32, d_pool):
    it = iter(refs)
    x_ref, mod_ref, g1_ref, w_ref, qg_ref, kg_ref = (next(it) for _ in range(6))
    cos_ref = sin_ref = None
    if use_rope:
        cos_ref, sin_ref = next(it), next(it)
    h_ref, u_ref, q_ref, k_ref, v_ref = (next(it) for _ in range(5))
    kf_ref = vf_ref = None
    if emit_kv_f32:
        kf_ref, vf_ref = next(it), next(it)

    x = x_ref[...]
    ms = jnp.mean(x * x, axis=-1, keepdims=True)
    xn = x * lax.rsqrt(ms + EPS) * g1_ref[...]
    shift = mod_ref[0, 0:1, :]
    scale = mod_ref[0, 1:2, :]
    hb = (xn * (1.0 + scale) + shift).astype(BF16)
    h_ref[...] = hb

    d_q = N_HEADS * HEAD_DIM
    d_kv = N_KV_HEADS * HEAD_DIM
    chunk = Q_GROUP * HEAD_DIM

    for c0 in range(0, d_pool, chunk):
        u_ref[:, c0:c0 + chunk] = jnp.dot(hb, w_ref[:, c0:c0 + chunk], preferred_element_type=F32)

    if use_rope:
        cos = cos_ref[...]
        sin = sin_ref[...]

    def head(xh, g):
        hm = jnp.mean(xh * xh, axis=-1, keepdims=True)
        y = xh * lax.rsqrt(hm + EPS) * g
        if use_rope:
            y = y * cos + pltpu.roll(y, HEAD_DIM // 2, axis=1) * sin
        return y

    qg = qg_ref[...]
    kg = kg_ref[...]
    for c0 in range(0, d_q, chunk):
        pr = jnp.dot(hb, w_ref[:, d_pool + c0:d_pool + c0 + chunk], preferred_element_type=F32)
        for j in range(chunk // HEAD_DIM):
            yh = head(pr[:, j * HEAD_DIM:(j + 1) * HEAD_DIM], qg)
            q_ref[:, c0 + j * HEAD_DIM:c0 + (j + 1) * HEAD_DIM] = yh.astype(BF16)

    pk = jnp.dot(hb, w_ref[:, d_pool + d_q:d_pool + d_q + d_kv], preferred_element_type=F32)
    for j in range(N_KV_HEADS):
        yh = head(pk[:, j * HEAD_DIM:(j + 1) * HEAD_DIM], kg)
        k_ref[:, j * HEAD_DIM:(j + 1) * HEAD_DIM] = yh.astype(BF16)
        if emit_kv_f32:
            kf_ref[:, j * HEAD_DIM:(j + 1) * HEAD_DIM] = yh

    pv = jnp.dot(hb, w_ref[:, d_pool + d_q + d_kv:], preferred_element_type=F32)
    v_ref[...] = pv.astype(BF16)
    if emit_kv_f32:
        vf_ref[...] = pv


def _in_projection(x2d, seq, mod3, mod_row, g1, w_in_b, qg, kg, rope, emit_kv_f32):
    n, d = x2d.shape
    d_in = w_in_b.shape[1]
    d_q = N_HEADS * HEAD_DIM
    d_kv = N_KV_HEADS * HEAD_DIM
    d_pool = d_in - d_q - 2 * d_kv
    ts = _pick(seq, (512, 256, 128))
    tiles_per_seq = seq // ts
    use_rope = rope is not None

    in_specs = [
        pl.BlockSpec((ts, d), lambda i: (i, 0)),
        pl.BlockSpec((1, N_MOD, d), lambda i: (mod_row(i * ts), 0, 0)),
        _resident((1, d), lambda i: (0, 0)),
        _resident((d, d_in), lambda i: (0, 0)),
        _resident((1, HEAD_DIM), lambda i: (0, 0)),
        _resident((1, HEAD_DIM), lambda i: (0, 0)),
    ]
    args = [x2d, mod3, g1.reshape(1, d), w_in_b, qg.reshape(1, HEAD_DIM), kg.reshape(1, HEAD_DIM)]
    if use_rope:
        in_specs += [pl.BlockSpec((ts, HEAD_DIM), lambda i: (i % tiles_per_seq, 0))] * 2
        args += list(rope)

    out_shape = [
        jax.ShapeDtypeStruct((n, d), BF16),
        jax.ShapeDtypeStruct((n, d_pool), F32),
        jax.ShapeDtypeStruct((n, d_q), BF16),
        jax.ShapeDtypeStruct((n, d_kv), BF16),
        jax.ShapeDtypeStruct((n, d_kv), BF16),
    ]
    out_specs = [
        pl.BlockSpec((ts, d), lambda i: (i, 0)),
        pl.BlockSpec((ts, d_pool), lambda i: (i, 0)),
        pl.BlockSpec((ts, d_q), lambda i: (i, 0)),
        pl.BlockSpec((ts, d_kv), lambda i: (i, 0)),
        pl.BlockSpec((ts, d_kv), lambda i: (i, 0)),
    ]
    if emit_kv_f32:
        out_shape += [jax.ShapeDtypeStruct((n, d_kv), F32)] * 2
        out_specs += [pl.BlockSpec((ts, d_kv), lambda i: (i, 0))] * 2

    return pl.pallas_call(
        functools.partial(_inproj_kernel, use_rope=use_rope, emit_kv_f32=emit_kv_f32, d_pool=d_pool),
        out_shape=out_shape,
        grid=(n // ts,),
        in_specs=in_specs,
        out_specs=out_specs,
        compiler_params=_cparams(("arbitrary",)),
        name="in_projection",
    )(*args)


def _attn_kernel(*refs, tk, kv_chunks, scale, n_cast, n_zero):
    n_kv = len(kv_chunks)
    q_ref = refs[0]
    kv_refs = refs[1:1 + 2 * n_kv]
    refs = refs[1 + 2 * n_kv:]
    cast_in = refs[:n_cast]
    o_ref = refs[n_cast]
    cast_out = refs[1 + n_cast:1 + 2 * n_cast]
    zero_out = refs[1 + 2 * n_cast:1 + 2 * n_cast + n_zero]
    for src, dst in zip(cast_in, cast_out):
        dst[...] = src[...].astype(dst.dtype)
    for dst in zero_out:
        dst[...] = jnp.zeros_like(dst)

    q = q_ref[0]
    tq = q.shape[0]
    qs = jnp.concatenate([q[:, j * HEAD_DIM:(j + 1) * HEAD_DIM] for j in range(Q_GROUP)], axis=0)
    rows = qs.shape[0]
    c2 = scale * 1.4426950408889634
    m = jnp.full((rows, 1), -jnp.inf, F32)
    l = jnp.zeros((rows, 1), F32)
    acc = jnp.zeros((rows, HEAD_DIM), F32)
    chunks = [(kv_refs[2 * i], kv_refs[2 * i + 1], c) for i, n in enumerate(kv_chunks) for c in range(n)]
    for k_ref, v_ref, c in chunks:
        kc = k_ref[0, c * tk:(c + 1) * tk, :]
        vc = v_ref[0, c * tk:(c + 1) * tk, :]
        s = lax.dot_general(qs, kc, (((1,), (1,)), ((), ())), preferred_element_type=F32) * c2
        m_new = jnp.maximum(m, jnp.max(s, axis=1, keepdims=True))
        alpha = jnp.exp2(m - m_new)
        p = jnp.exp2(s - m_new)
        l = alpha * l + jnp.sum(p, axis=1, keepdims=True)
        acc = alpha * acc + jnp.dot(p.astype(BF16), vc, preferred_element_type=F32)
        m = m_new
    out = acc / l
    o_ref[0] = jnp.concatenate([out[j * tq:(j + 1) * tq] for j in range(Q_GROUP)], axis=1).astype(o_ref.dtype)


def _side_rows(total_rows, row_bytes, n_steps):
    if total_rows % n_steps:
        return None
    rows = total_rows // n_steps
    if rows % BF16_SUBLANES or rows * row_bytes > SIDE_BLOCK_BYTES:
        return None
    return rows


def _attention(q3, kv_sources, cast_jobs=(), zero_jobs=()):
    b, s, d_q = q3.shape
    kv_lens = [k3.shape[1] for k3, _ in kv_sources]
    tq = _pick(s, (512, 256, 128))
    tk = next(t for t in (256, 128) if all(n % t == 0 for n in kv_lens))
    gw = Q_GROUP * HEAD_DIM
    nq = s // tq
    n_steps = b * N_KV_HEADS * nq

    def step(bi, g, qi):
        return ((bi * N_KV_HEADS + g) * nq + qi, 0)

    cast_rows = [_side_rows(a.shape[0], a.shape[1] * 4, n_steps) for a in cast_jobs]
    zero_rows = [_side_rows(r, c * 4, n_steps) for r, c in zero_jobs]
    fused_cast = [a for a, r in zip(cast_jobs, cast_rows) if r is not None]
    fused_zero = [z for z, r in zip(zero_jobs, zero_rows) if r is not None]

    in_specs = [pl.BlockSpec((1, tq, gw), lambda bi, g, qi: (bi, qi, g))]
    for n in kv_lens:
        in_specs += [pl.BlockSpec((1, n, HEAD_DIM), lambda bi, g, qi: (bi, 0, g))] * 2
    out_shape = [jax.ShapeDtypeStruct((b, s, d_q), BF16)]
    out_specs = [pl.BlockSpec((1, tq, gw), lambda bi, g, qi: (bi, qi, g))]
    for a, r in zip(cast_jobs, cast_rows):
        if r is not None:
            in_specs.append(pl.BlockSpec((r, a.shape[1]), step))
            out_shape.append(jax.ShapeDtypeStruct(a.shape, BF16))
            out_specs.append(pl.BlockSpec((r, a.shape[1]), step))
    for (rows, cols), r in zip(zero_jobs, zero_rows):
        if r is not None:
            out_shape.append(jax.ShapeDtypeStruct((rows, cols), F32))
            out_specs.append(pl.BlockSpec((r, cols), step))

    outs = pl.pallas_call(
        functools.partial(_attn_kernel, tk=tk, kv_chunks=tuple(n // tk for n in kv_lens),
                          scale=HEAD_DIM ** -0.5, n_cast=len(fused_cast), n_zero=len(fused_zero)),
        out_shape=out_shape,
        grid=(b, N_KV_HEADS, nq),
        in_specs=in_specs,
        out_specs=out_specs,
        compiler_params=_cparams(("arbitrary", "arbitrary", "arbitrary")),
        name="attention",
    )(q3, *[a for kv in kv_sources for a in kv], *fused_cast)

    it = iter(outs[1:])
    casts = [next(it) if r is not None else a.astype(BF16) for a, r in zip(cast_jobs, cast_rows)]
    zeros = [next(it) if r is not None else jnp.zeros(z, F32) for z, r in zip(zero_jobs, zero_rows)]
    return outs[0], casts, zeros


def _pool_kernel(up_ref, u_ref, un_ref, wp_ref, ps_ref, o_ref, ext_ref, *, tp, seq):
    i = pl.program_id(0)
    pos0 = (i * tp) % seq
    has_prev = pos0 > 0
    has_next = pos0 + tp < seq
    h = POOL_HALO
    ext_ref[0:h, :] = jnp.where(has_prev, up_ref[...], 0.0)
    ext_ref[h:h + tp, :] = u_ref[...]
    ext_ref[h + tp:h + tp + h, :] = jnp.where(has_next, un_ref[...], 0.0)

    gdim = wp_ref.shape[1]
    pos = pos0 + lax.broadcasted_iota(I32, (tp, 1), 0)
    for g, w in enumerate(POOL_WINDOWS):
        c0 = g * gdim
        acc = jnp.zeros((tp, gdim), F32)
        for j in range(-(w // 2), w - w // 2):
            acc = acc + ext_ref[h + j:h + j + tp, c0:c0 + gdim]
        lo = jnp.maximum(pos - w // 2, 0)
        hi = jnp.minimum(pos + (w - w // 2), seq)
        mean = acc / (hi - lo).astype(F32)
        pooled = (mean - u_ref[:, c0:c0 + gdim]).astype(BF16)
        y = jnp.dot(pooled, wp_ref[g], preferred_element_type=F32) * ps_ref[:, c0:c0 + gdim]
        o_ref[:, c0:c0 + gdim] = y.astype(o_ref.dtype)


def _pool_mixer(u2d, seq, w_pool_b, pool_scale):
    n, d_pool = u2d.shape
    tp = _pick(seq, (512, 256, 128))
    h = POOL_HALO
    assert max(POOL_WINDOWS) // 2 <= h
    nb = n // h
    per = tp // h
    return pl.pallas_call(
        functools.partial(_pool_kernel, tp=tp, seq=seq),
        out_shape=jax.ShapeDtypeStruct((n, d_pool), BF16),
        grid=(n // tp,),
        in_specs=[
            pl.BlockSpec((h, d_pool), lambda i: (jnp.maximum(i * per - 1, 0), 0)),
            pl.BlockSpec((tp, d_pool), lambda i: (i, 0)),
            pl.BlockSpec((h, d_pool), lambda i: (jnp.minimum((i + 1) * per, nb - 1), 0)),
            _resident(w_pool_b.shape, lambda i: (0, 0, 0)),
            _resident((1, d_pool), lambda i: (0, 0)),
        ],
        out_specs=pl.BlockSpec((tp, d_pool), lambda i: (i, 0)),
        scratch_shapes=[pltpu.VMEM((tp + 2 * h, d_pool), F32)],
        compiler_params=_cparams(("arbitrary",)),
        name="pool_mixer",
    )(u2d, u2d, u2d, w_pool_b, pool_scale.reshape(1, d_pool))


def _merge_kernel(h_ref, pm_ref, at_ref, wga_ref, wgb_ref, bga_ref, bgb_ref, wa_ref, wb_ref, o_ref):
    hb = h_ref[...]
    ga = jax.nn.sigmoid(jnp.dot(hb, wga_ref[...], preferred_element_type=F32) + bga_ref[...])
    gb = jax.nn.sigmoid(jnp.dot(hb, wgb_ref[...], preferred_element_type=F32) + bgb_ref[...])
    ya = jnp.dot(pm_ref[...], wa_ref[...], preferred_element_type=F32)
    yb = jnp.dot(at_ref[...], wb_ref[...], preferred_element_type=F32)
    o_ref[...] = (ga * ya + gb * yb).astype(o_ref.dtype)


def _gated_merge(h2d, pm2d, at2d, w_gate_b, b_gate, w_a_b, w_b_b):
    n, d = h2d.shape
    d_pool = pm2d.shape[1]
    d_q = at2d.shape[1]
    tm = _pick(n, (1024, 512, 256, 128))
    tn = _pick(d, (512, 256, 128))
    nj = d // tn
    bg = b_gate.reshape(1, 2 * d)
    return pl.pallas_call(
        _merge_kernel,
        out_shape=jax.ShapeDtypeStruct((n, d), BF16),
        grid=(n // tm, nj),
        in_specs=[
            pl.BlockSpec((tm, d), lambda i, j: (i, 0)),
            pl.BlockSpec((tm, d_pool), lambda i, j: (i, 0)),
            pl.BlockSpec((tm, d_q), lambda i, j: (i, 0)),
            pl.BlockSpec((d, tn), lambda i, j: (0, j)),
            pl.BlockSpec((d, tn), lambda i, j: (0, nj + j)),
            pl.BlockSpec((1, tn), lambda i, j: (0, j)),
            pl.BlockSpec((1, tn), lambda i, j: (0, nj + j)),
            pl.BlockSpec((d_pool, tn), lambda i, j: (0, j)),
            pl.BlockSpec((d_q, tn), lambda i, j: (0, j)),
        ],
        out_specs=pl.BlockSpec((tm, tn), lambda i, j: (i, j)),
        compiler_params=_cparams(("arbitrary", "arbitrary")),
        name="gated_merge",
    )(h2d, pm2d, at2d, w_gate_b, w_gate_b, bg, bg, w_a_b, w_b_b)


def _outproj_kernel(mg_ref, x_ref, mod_ref, wo_ref, g2_ref, wr_ref, br_ref, x1_ref, h2_ref, lg_ref):
    mix = jnp.dot(mg_ref[...], wo_ref[...], preferred_element_type=F32)
    gate1 = mod_ref[0, 2:3, :]
    shift2 = mod_ref[0, 3:4, :]
    scale2 = mod_ref[0, 4:5, :]
    x1 = x_ref[...] + gate1 * mix
    x1_ref[...] = x1
    ms = jnp.mean(x1 * x1, axis=-1, keepdims=True)
    h2 = (x1 * lax.rsqrt(ms + EPS) * g2_ref[...]) * (1.0 + scale2) + shift2
    h2_ref[...] = h2
    lg_ref[...] = jnp.dot(h2.astype(BF16), wr_ref[...], preferred_element_type=F32) + br_ref[...]


def _out_projection(mg2d, x2d, mod3, mod_row, w_o_b, g2, w_router, b_router):
    n, d = x2d.shape
    ne = w_router.shape[1]
    tm = _pick(n, (256, 128))
    return pl.pallas_call(
        _outproj_kernel,
        out_shape=[
            jax.ShapeDtypeStruct((n, d), F32),
            jax.ShapeDtypeStruct((n, d), F32),
            jax.ShapeDtypeStruct((n, ne), F32),
        ],
        grid=(n // tm,),
        in_specs=[
            pl.BlockSpec((tm, d), lambda i: (i, 0)),
            pl.BlockSpec((tm, d), lambda i: (i, 0)),
            pl.BlockSpec((1, N_MOD, d), lambda i: (mod_row(i * tm), 0, 0)),
            _resident((d, d), lambda i: (0, 0)),
            _resident((1, d), lambda i: (0, 0)),
            _resident((d, ne), lambda i: (0, 0)),
            _resident((1, ne), lambda i: (0, 0)),
        ],
        out_specs=[
            pl.BlockSpec((tm, d), lambda i: (i, 0)),
            pl.BlockSpec((tm, d), lambda i: (i, 0)),
            pl.BlockSpec((tm, ne), lambda i: (i, 0)),
        ],
        compiler_params=_cparams(("arbitrary",)),
        name="out_projection",
    )(mg2d, x2d, mod3, w_o_b, g2.reshape(1, d), w_router, b_router.reshape(1, ne))


def _route_kernel(lg_ref, idx_ref, gw_ref, rk_ref, cnt_ref):
    step = pl.program_id(0)

    @pl.when(step == 0)
    def _():
        cnt_ref[...] = jnp.zeros_like(cnt_ref)

    lg = lg_ref[...]
    tr, ne = lg.shape
    lane = lax.broadcasted_iota(I32, (tr, ne), 1)
    work = lg
    vals, idxs = [], []
    member = jnp.zeros((tr, ne), F32)
    for _ in range(TOP_K):
        mx = jnp.max(work, axis=-1, keepdims=True)
        ix = jnp.min(jnp.where(work == mx, lane, ne), axis=-1, keepdims=True)
        sel = lane == ix
        vals.append(mx)
        idxs.append(ix)
        member = jnp.where(sel, 1.0, member)
        work = jnp.where(sel, -jnp.inf, work)

    ex = [jnp.exp(v - vals[0]) for v in vals]
    den = ex[0]
    for e in ex[1:]:
        den = den + e

    row = lax.broadcasted_iota(I32, (tr, tr), 0)
    col = lax.broadcasted_iota(I32, (tr, tr), 1)
    lower = jnp.where(col < row, 1.0, 0.0).astype(BF16)
    before = jnp.dot(lower, member.astype(BF16), preferred_element_type=F32) + cnt_ref[...]

    k_lane = lax.broadcasted_iota(I32, (tr, TOP_K), 1)
    idx_out = jnp.zeros((tr, TOP_K), I32)
    gw_out = jnp.zeros((tr, TOP_K), F32)
    rk_out = jnp.zeros((tr, TOP_K), I32)
    for k in range(TOP_K):
        rk = jnp.sum(jnp.where(lane == idxs[k], before, 0.0), axis=-1, keepdims=True).astype(I32)
        idx_out = jnp.where(k_lane == k, idxs[k], idx_out)
        gw_out = jnp.where(k_lane == k, ex[k] / den, gw_out)
        rk_out = jnp.where(k_lane == k, rk, rk_out)
    idx_ref[...] = idx_out
    gw_ref[...] = gw_out
    rk_ref[...] = rk_out
    cnt_ref[...] += jnp.sum(member, axis=0, keepdims=True)


def _routing(logits):
    n, ne = logits.shape
    tr = _pick(n, (512, 256, 128))
    return pl.pallas_call(
        _route_kernel,
        out_shape=[
            jax.ShapeDtypeStruct((n, TOP_K), I32),
            jax.ShapeDtypeStruct((n, TOP_K), F32),
            jax.ShapeDtypeStruct((n, TOP_K), I32),
            jax.ShapeDtypeStruct((1, ne), F32),
        ],
        grid=(n // tr,),
        in_specs=[pl.BlockSpec((tr, ne), lambda i: (i, 0))],
        out_specs=[
            pl.BlockSpec((tr, TOP_K), lambda i: (i, 0)),
            pl.BlockSpec((tr, TOP_K), lambda i: (i, 0)),
            pl.BlockSpec((tr, TOP_K), lambda i: (i, 0)),
            pl.BlockSpec((1, ne), lambda i: (0, 0)),
        ],
        compiler_params=_cparams(("arbitrary",)),
        name="routing",
    )(logits)


def _dispatch_kernel(dest_ref, h_ref, xs_in_hbm, xs_hbm, sem, *, td):
    del xs_in_hbm

    def start(r, carry):
        for k in range(TOP_K):
            d = dest_ref[0, 0, r * TOP_K + k]
            pltpu.make_async_copy(h_ref.at[pl.ds(r, 1)], xs_hbm.at[pl.ds(d, 1)], sem).start()
        return carry

    lax.fori_loop(0, td, start, 0)
    for _ in range(TOP_K):
        pltpu.make_async_copy(h_ref, xs_hbm.at[pl.ds(0, td)], sem).wait()


def _dispatch(h2d, dest, row0, xs0):
    n, d = h2d.shape
    n_slots = xs0.shape[0]
    td = _pick(math.gcd(n, row0) if row0 else n, (512, 256, 128))
    assert dest.shape[0] % td == 0
    dest3 = dest.reshape(dest.shape[0] // td, 1, td * TOP_K)
    blk0 = row0 // td
    return pl.pallas_call(
        functools.partial(_dispatch_kernel, td=td),
        out_shape=jax.ShapeDtypeStruct((n_slots, d), h2d.dtype),
        grid=(n // td,),
        in_specs=[
            pl.BlockSpec((1, 1, td * TOP_K), lambda i: (blk0 + i, 0, 0), memory_space=pltpu.SMEM),
            pl.BlockSpec((td, d), lambda i: (i, 0)),
            pl.BlockSpec(memory_space=pl.ANY),
        ],
        out_specs=pl.BlockSpec(memory_space=pl.ANY),
        scratch_shapes=[pltpu.SemaphoreType.DMA(())],
        input_output_aliases={2: 0},
        compiler_params=_cparams(("arbitrary",)),
        name="dispatch",
    )(dest3, h2d, xs0)


def _expert_kernel(te_ref, tr_ref, nu_ref, x_ref, wg_ref, wl_ref, bg_ref, bl_ref, wd_ref, bd_ref, y_ref, *, nf):
    del te_ref
    i = pl.program_id(0)
    f = pl.program_id(1)
    used = i < nu_ref[0]
    rows = tr_ref[jnp.minimum(i, nu_ref[0] - 1)]
    te = x_ref.shape[0]
    half = te // 2

    def mlp(m, first):
        xb = x_ref[0:m, :].astype(BF16)
        g = jnp.dot(xb, wg_ref[0], preferred_element_type=F32) + bg_ref[0]
        lin = jnp.dot(xb, wl_ref[0], preferred_element_type=F32) + bl_ref[0]
        g = jnp.minimum(g, SWIGLU_LIMIT)
        lin = jnp.clip(lin, -SWIGLU_LIMIT, SWIGLU_LIMIT)
        act = g * jax.nn.sigmoid(SWIGLU_ALPHA * g) * (lin + 1.0)
        down = jnp.dot(act.astype(BF16), wd_ref[0], preferred_element_type=F32)
        if first:
            y_ref[0:m, :] = down + bd_ref[0]
            if m < te:
                y_ref[m:te, :] = jnp.zeros((te - m, y_ref.shape[1]), F32)
        else:
            y_ref[0:m, :] += down

    for m, occupied in ((te, rows > half), (half, rows <= half)):
        for first in (True, False):
            at_step = (f == 0) if first else (f != 0)

            @pl.when(jnp.logical_and(jnp.logical_and(used, occupied), at_step))
            def _():
                mlp(m, first)

    @pl.when(jnp.logical_and(jnp.logical_not(used), f == nf - 1))
    def _():
        y_ref[...] = jnp.zeros_like(y_ref)


def _expert_mlp(xs, tile_expert, tile_rows, n_used, w_gu_b, b_gu, w_down_b, b_down, te):
    n_slots, d = xs.shape
    ne, _, d_ff2 = w_gu_b.shape
    d_ff = d_ff2 // 2
    tf = _pick(d_ff, (1024, 512, 256, 128))
    nf = d_ff // tf
    n_tiles = n_slots // te

    def tile(i, nu):
        return jnp.minimum(i, nu[0] - 1)

    def ff(i, f, nu):
        return jnp.where(i < nu[0], f, nf - 1)

    grid_spec = pltpu.PrefetchScalarGridSpec(
        num_scalar_prefetch=3,
        grid=(n_tiles, nf),
        in_specs=[
            pl.BlockSpec((te, d), lambda i, f, tx, tr, nu: (tile(i, nu), 0)),
            pl.BlockSpec((1, d, tf), lambda i, f, tx, tr, nu: (tx[tile(i, nu)], 0, ff(i, f, nu))),
            pl.BlockSpec((1, d, tf), lambda i, f, tx, tr, nu: (tx[tile(i, nu)], 0, nf + ff(i, f, nu))),
            pl.BlockSpec((1, 1, tf), lambda i, f, tx, tr, nu: (tx[tile(i, nu)], 0, ff(i, f, nu))),
            pl.BlockSpec((1, 1, tf), lambda i, f, tx, tr, nu: (tx[tile(i, nu)], 0, nf + ff(i, f, nu))),
            pl.BlockSpec((1, tf, d), lambda i, f, tx, tr, nu: (tx[tile(i, nu)], ff(i, f, nu), 0)),
            pl.BlockSpec((1, 1, d), lambda i, f, tx, tr, nu: (tx[tile(i, nu)], 0, 0)),
        ],
        out_specs=pl.BlockSpec((te, d), lambda i, f, tx, tr, nu: (i, 0)),
    )
    return pl.pallas_call(
        functools.partial(_expert_kernel, nf=nf),
        out_shape=jax.ShapeDtypeStruct((n_slots, d), F32),
        grid_spec=grid_spec,
        compiler_params=_cparams(("arbitrary", "arbitrary")),
        name="expert_mlp",
    )(tile_expert, tile_rows, n_used, xs, w_gu_b, w_gu_b, b_gu.reshape(ne, 1, d_ff2), b_gu.reshape(ne, 1, d_ff2),
      w_down_b, b_down.reshape(ne, 1, d))


def _combine_kernel(dcur_ref, dnext_ref, x1_ref, gw_ref, mod_ref, gf_ref, y_hbm, o_ref, ybuf, sems, *,
                    tc, n_steps):
    i = pl.program_id(0)
    slot = i % 2

    def issue(dest_ref, s):
        def start(r, carry):
            for k in range(TOP_K):
                d = dest_ref[0, 0, r * TOP_K + k]
                pltpu.make_async_copy(y_hbm.at[pl.ds(d, 1)], ybuf.at[s, k, pl.ds(r, 1)], sems.at[s]).start()
            return carry

        lax.fori_loop(0, tc, start, 0)

    @pl.when(i == 0)
    def _():
        issue(dcur_ref, 0)

    for s in range(2):
        @pl.when(jnp.logical_and(i + 1 < n_steps, slot == 1 - s))
        def _():
            issue(dnext_ref, s)

    for k in range(TOP_K):
        pltpu.make_async_copy(y_hbm.at[pl.ds(0, tc)], ybuf.at[slot, k], sems.at[slot]).wait()

    gw = gw_ref[...]
    moe = gw[:, 0:1] * ybuf[slot, 0]
    for k in range(1, TOP_K):
        moe = moe + gw[:, k:k + 1] * ybuf[slot, k]
    gate2 = mod_ref[0, 5:6, :]
    x2 = x1_ref[...] + gate2 * moe
    ms = jnp.mean(x2 * x2, axis=-1, keepdims=True)
    o_ref[...] = x2 * lax.rsqrt(ms + EPS) * gf_ref[...]


def _combine(x1, gate_w, dest, row0, y, mod3, mod_row, final_g):
    n, d = x1.shape
    tc = _pick(math.gcd(n, row0) if row0 else n, (256, 128))
    assert dest.shape[0] % tc == 0
    n_steps = n // tc
    dest3 = dest.reshape(dest.shape[0] // tc, 1, tc * TOP_K)
    blk0 = row0 // tc
    return pl.pallas_call(
        functools.partial(_combine_kernel, tc=tc, n_steps=n_steps),
        out_shape=jax.ShapeDtypeStruct((n, d), F32),
        grid=(n_steps,),
        in_specs=[
            pl.BlockSpec((1, 1, tc * TOP_K), lambda i: (blk0 + i, 0, 0), memory_space=pltpu.SMEM),
            pl.BlockSpec((1, 1, tc * TOP_K), lambda i: (blk0 + jnp.minimum(i + 1, n_steps - 1), 0, 0),
                         memory_space=pltpu.SMEM),
            pl.BlockSpec((tc, d), lambda i: (i, 0)),
            pl.BlockSpec((tc, TOP_K), lambda i: (blk0 + i, 0)),
            pl.BlockSpec((1, N_MOD, d), lambda i: (mod_row(i * tc), 0, 0)),
            _resident((1, d), lambda i: (0, 0)),
            pl.BlockSpec(memory_space=pl.ANY),
        ],
        out_specs=pl.BlockSpec((tc, d), lambda i: (i, 0)),
        scratch_shapes=[pltpu.VMEM((2, TOP_K, tc, d), F32), pltpu.SemaphoreType.DMA((2,))],
        compiler_params=_cparams(("arbitrary",)),
        name="combine",
    )(dest3, dest3, x1, gate_w, mod3, final_g.reshape(1, d), y)


def _rope_tables(seq):
    rows = seq // GRID_W
    row = jnp.repeat(jnp.arange(rows), GRID_W).astype(F32)
    col = jnp.tile(jnp.arange(GRID_W), rows).astype(F32)
    freqs = ROPE_THETA ** (-jnp.arange(ROPE_PAIRS_PER_AXIS, dtype=F32) / ROPE_PAIRS_PER_AXIS)
    ang = jnp.concatenate([row[:, None] * freqs, col[:, None] * freqs], axis=-1)
    cos, sin = jnp.cos(ang), jnp.sin(ang)
    return jnp.concatenate([cos, cos], axis=-1), jnp.concatenate([-sin, sin], axis=-1)


def _slot_tiles(n_tokens, n_experts):
    return (n_tokens * TOP_K) // EXPERT_TILE_ROWS + n_experts


def _moe(groups, wts, final_g, mod3, xs0):
    te = EXPERT_TILE_ROWS
    sizes = [g[0].shape[0] for g in groups]
    n = sum(sizes)
    logits = jnp.concatenate([g[2] for g in groups], axis=0)
    ne = logits.shape[1]
    top_i, gate_w, rank, counts = _routing(logits)

    counts = counts.reshape(ne).astype(I32)
    padded = (counts + te - 1) // te * te
    pad_end = jnp.cumsum(padded)
    pad_start = pad_end - padded
    n_tiles = _slot_tiles(n, ne)
    assert xs0.shape[0] == n_tiles * te
    tile_start = jnp.arange(n_tiles, dtype=I32) * te
    tile_expert = jnp.minimum(jnp.sum((pad_end[None, :] <= tile_start[:, None]).astype(I32), axis=1), ne - 1)
    tile_rows = jnp.clip(counts[tile_expert] - (tile_start - pad_start[tile_expert]), 0, te).astype(I32)
    n_used = (pad_end[-1:] // te).astype(I32)
    onehot = top_i[:, :, None] == jnp.arange(ne, dtype=I32)
    dest = jnp.sum(jnp.where(onehot, pad_start, 0), axis=-1) + rank

    offs = [sum(sizes[:i]) for i in range(len(sizes))]
    xs = xs0
    for (_, h2, _, _), o in zip(groups, offs):
        xs = _dispatch(h2, dest, o, xs)
    y = _expert_mlp(xs, tile_expert, tile_rows, n_used, wts["w_gu"], wts["b_gu"], wts["w_down"], wts["b_down"], te)
    return [_combine(x1, gate_w, dest, o, y, mod3, mod_row, final_g)
            for (x1, _, _, mod_row), o in zip(groups, offs)]


def _mixer(x2d, batch, seq, mod3, mod_row, rope, kv_ctx, wts, emit_kv_f32, cast_jobs=(), zero_jobs=()):
    outs = _in_projection(x2d, seq, mod3, mod_row, wts["norm1_g"], wts["w_in"], wts["q_norm_g"],
                          wts["k_norm_g"], rope, emit_kv_f32)
    h, u, q, k, v = outs[:5]
    d_q = q.shape[1]
    d_kv = k.shape[1]
    q3 = q.reshape(batch, seq, d_q)
    k3 = k.reshape(batch, seq, d_kv)
    v3 = v.reshape(batch, seq, d_kv)
    kv_sources = ([kv_ctx] if kv_ctx is not None else []) + [(k3, v3)]
    attn, casts, zeros = _attention(q3, kv_sources, cast_jobs, zero_jobs)
    attn = attn.reshape(batch * seq, d_q)
    pm = _pool_mixer(u, seq, wts["w_pool"], wts["pool_scale"])
    merged = _gated_merge(h, pm, attn, wts["w_gate"], wts["b_gate"], wts["w_a_out"], wts["w_b_out"])
    x1, h2, logits = _out_projection(merged, x2d, mod3, mod_row, wts["w_o"], wts["norm2_g"],
                                     wts["w_router"], wts["b_router"])
    return (x1, h2, logits, mod_row), outs[5:], casts, zeros


def kernel(x_prompt, x_sample, cache_k, cache_v, c, c_ctx, w_mod, b_mod, norm1_g, w_in, q_norm_g, k_norm_g, w_pool, pool_scale, w_a_out, w_b_out, w_gate, b_gate, w_o, norm2_g, w_router, b_router, w_gu, b_gu, w_down, b_down, final_g):
    depth = w_mod.shape[0]
    assert depth == 1, "the final norm is fused into the last combine; one trunk layer is supported"
    bp, sp, d = x_prompt.shape
    bs, ss, _ = x_sample.shape
    l = 0

    n_rows = -(-(bs + 1) // SUBLANES) * SUBLANES
    cvec = jnp.zeros((n_rows, d), F32).at[:bs].set(c).at[bs].set(c_ctx)
    mod3 = _modulation(cvec, w_mod[l], b_mod[l]).reshape(n_rows, N_MOD, d)

    wts = dict(
        norm1_g=norm1_g[l], w_in=w_in[l].astype(BF16), q_norm_g=q_norm_g[l], k_norm_g=k_norm_g[l],
        w_pool=w_pool[l].astype(BF16), pool_scale=pool_scale[l], w_a_out=w_a_out[l].astype(BF16),
        w_b_out=w_b_out[l].astype(BF16), w_gate=w_gate[l].astype(BF16), b_gate=b_gate[l],
        w_o=w_o[l].astype(BF16), norm2_g=norm2_g[l], w_router=w_router[l].astype(BF16), b_router=b_router[l],
        b_gu=b_gu[l], b_down=b_down[l],
    )

    gp, (kf, vf), _, _ = _mixer(x_prompt.reshape(bp * sp, d), bp, sp, mod3, lambda t: bs, None, None,
                                wts, True)
    kv_ctx = (cache_k[:, l].reshape(bs, -1, N_KV_HEADS * HEAD_DIM).astype(BF16),
              cache_v[:, l].reshape(bs, -1, N_KV_HEADS * HEAD_DIM).astype(BF16))
    ne, _, d_ff2 = w_gu[l].shape
    n_slots = _slot_tiles(bp * sp + bs * ss, ne) * EXPERT_TILE_ROWS
    gs, _, (w_gu_b, w_down_b), (xs0,) = _mixer(
        x_sample.reshape(bs * ss, d), bs, ss, mod3, lambda t: t // ss, _rope_tables(ss), kv_ctx, wts, False,
        cast_jobs=(w_gu[l].reshape(ne * d, d_ff2), w_down[l].reshape(-1, d)), zero_jobs=((n_slots, d),))
    wts["w_gu"] = w_gu_b.reshape(ne, d, d_ff2)
    wts["w_down"] = w_down_b.reshape(w_down[l].shape)
    yp, ys = _moe([gp, gs], wts, final_g, mod3, xs0)

    y_prompt = yp.reshape(bp, sp, d)
    y_sample = ys.reshape(bs, ss, d)
    new_cache_k = kf.reshape(bp, 1, sp, N_KV_HEADS, HEAD_DIM)
    new_cache_v = vf.reshape(bp, 1, sp, N_KV_HEADS, HEAD_DIM)
    return (y_prompt, y_sample, new_cache_k, new_cache_v)
```

```python
import functools
import math

import jax
import jax.numpy as jnp
from jax import lax
from jax.experimental import pallas as pl
from jax.experimental.pallas import tpu as pltpu

F32 = jnp.float32
BF16 = jnp.bfloat16
I32 = jnp.int32

GRID_W = 64
N_HEADS = 16
N_KV_HEADS = 4
HEAD_DIM = 128
Q_GROUP = N_HEADS // N_KV_HEADS
ROPE_PAIRS_PER_AXIS = HEAD_DIM // 4
ROPE_THETA = 10000.0
POOL_WINDOWS = (2, 4, 8, 16)
TOP_K = 4
SWIGLU_LIMIT = 7.0
SWIGLU_ALPHA = 1.702
N_MOD = 6
EPS = 1e-6

LANES = 128
SUBLANES = 8
BF16_SUBLANES = 16
VMEM_LIMIT_BYTES = 56 * 1024 * 1024
SIDE_BLOCK_BYTES = 6 * 1024 * 1024

IN_PROJ_SUB_BLOCKS = 2
EXPERT_TILE_ROWS = 512
POOL_HALO = SUBLANES


def _cparams(semantics):
    return pltpu.CompilerParams(dimension_semantics=semantics, vmem_limit_bytes=VMEM_LIMIT_BYTES)


def _pick(n, candidates):
    for c in candidates:
        if n % c == 0:
            return c
    raise ValueError(f"no tile in {candidates} divides {n}")


def _resident(shape, index_map):
    return pl.BlockSpec(shape, index_map, pipeline_mode=pl.Buffered(1))


def _mod_kernel(c_ref, w_ref, b_ref, o_ref):
    c = c_ref[...]
    s = (c * jax.nn.sigmoid(c)).astype(BF16)
    o_ref[...] = jnp.dot(s, w_ref[...].astype(BF16), preferred_element_type=F32) + b_ref[...]


def _modulation(cvec, w_mod, b_mod):
    r, d = cvec.shape
    n6 = w_mod.shape[1]
    tn = _pick(n6, (1024, 512, 256, 128))
    return pl.pallas_call(
        _mod_kernel,
        out_shape=jax.ShapeDtypeStruct((r, n6), F32),
        grid=(n6 // tn,),
        in_specs=[
            pl.BlockSpec((r, d), lambda j: (0, 0)),
            pl.BlockSpec((d, tn), lambda j: (0, j)),
            pl.BlockSpec((1, tn), lambda j: (0, j)),
        ],
        out_specs=pl.BlockSpec((r, tn), lambda j: (0, j)),
        compiler_params=_cparams(("arbitrary",)),
        name="modulation",
    )(cvec, w_mod, b_mod.reshape(1, n6))


def _inproj_kernel(*refs, use_rope, emit_kv_f32, d_pool):
    row_tiled = [True] + [False] * 5 + [True] * (len(refs) - 6)
    rows = refs[0].shape[0] // IN_PROJ_SUB_BLOCKS
    for sb in range(IN_PROJ_SUB_BLOCKS):
        views = [r.at[pl.ds(sb * rows, rows)] if t else r for r, t in zip(refs, row_tiled)]
        _inproj_rows(*views, use_rope=use_rope, emit_kv_f32=emit_kv_f32, d_pool=d_pool)


def _inproj_rows(*refs, use_rope, emit_kv_f32, d_pool):
    it = iter(refs)
    x_ref, mod_ref, g1_ref, w_ref, qg_ref, kg_ref = (next(it) for _ in range(6))
    cos_ref = sin_ref = None
    if use_rope:
        cos_ref, sin_ref = next(it), next(it)
    h_ref, u_ref, q_ref, k_ref, v_ref = (next(it) for _ in range(5))
    kf_ref = vf_ref = None
    if emit_kv_f32:
        kf_ref, vf_ref = next(it), next(it)

    x = x_ref[...]
    ms = jnp.mean(x * x, axis=-1, keepdims=True)
    xn = x * lax.rsqrt(ms + EPS) * g1_ref[...]
    shift = mod_ref[0, 0:1, :]
    scale = mod_ref[0, 1:2, :]
    hb = (xn * (1.0 + scale) + shift).astype(BF16)
    h_ref[...] = hb

    d_q = N_HEADS * HEAD_DIM
    d_kv = N_KV_HEADS * HEAD_DIM
    chunk = Q_GROUP * HEAD_DIM

    for c0 in range(0, d_pool, chunk):
        u_ref[:, c0:c0 + chunk] = jnp.dot(hb, w_ref[:, c0:c0 + chunk], preferred_element_type=F32)

    if use_rope:
        cos = cos_ref[...]
        sin = sin_ref[...]

    def head(xh, g):
        hm = jnp.mean(xh * xh, axis=-1, keepdims=True)
        y = xh * lax.rsqrt(hm + EPS) * g
        if use_rope:
            y = y * cos + pltpu.roll(y, HEAD_DIM // 2, axis=1) * sin
        return y

    qg = qg_ref[...]
    kg = kg_ref[...]
    for c0 in range(0, d_q, chunk):
        pr = jnp.dot(hb, w_ref[:, d_pool + c0:d_pool + c0 + chunk], preferred_element_type=F32)
        for j in range(chunk // HEAD_DIM):
            yh = head(pr[:, j * HEAD_DIM:(j + 1) * HEAD_DIM], qg)
            q_ref[:, c0 + j * HEAD_DIM:c0 + (j + 1) * HEAD_DIM] = yh.astype(BF16)

    pk = jnp.dot(hb, w_ref[:, d_pool + d_q:d_pool + d_q + d_kv], preferred_element_type=F32)
    for j in range(N_KV_HEADS):
        yh = head(pk[:, j * HEAD_DIM:(j + 1) * HEAD_DIM], kg)
        k_ref[:, j * HEAD_DIM:(j + 1) * HEAD_DIM] = yh.astype(BF16)
        if emit_kv_f32:
            kf_ref[:, j * HEAD_DIM:(j + 1) * HEAD_DIM] = yh

    pv = jnp.dot(hb, w_ref[:, d_pool + d_q + d_kv:], preferred_element_type=F32)
    v_ref[...] = pv.astype(BF16)
    if emit_kv_f32:
        vf_ref[...] = pv


def _in_projection(x2d, seq, mod3, mod_row, g1, w_in_b, qg, kg, rope, emit_kv_f32):
    n, d = x2d.shape
    d_in = w_in_b.shape[1]
    d_q = N_HEADS * HEAD_DIM
    d_kv = N_KV_HEADS * HEAD_DIM
    d_pool = d_in - d_q - 2 * d_kv
    ts = _pick(seq, (512, 256, 128))
    tiles_per_seq = seq // ts
    use_rope = rope is not None

    in_specs = [
        pl.BlockSpec((ts, d), lambda i: (i, 0)),
        pl.BlockSpec((1, N_MOD, d), lambda i: (mod_row(i * ts), 0, 0)),
        _resident((1, d), lambda i: (0, 0)),
        _resident((d, d_in), lambda i: (0, 0)),
        _resident((1, HEAD_DIM), lambda i: (0, 0)),
        _resident((1, HEAD_DIM), lambda i: (0, 0)),
    ]
    args = [x2d, mod3, g1.reshape(1, d), w_in_b, qg.reshape(1, HEAD_DIM), kg.reshape(1, HEAD_DIM)]
    if use_rope:
        in_specs += [pl.BlockSpec((ts, HEAD_DIM), lambda i: (i % tiles_per_seq, 0))] * 2
        args += list(rope)

    out_shape = [
        jax.ShapeDtypeStruct((n, d), BF16),
        jax.ShapeDtypeStruct((n, d_pool), F32),
        jax.ShapeDtypeStruct((n, d_q), BF16),
        jax.ShapeDtypeStruct((n, d_kv), BF16),
        jax.ShapeDtypeStruct((n, d_kv), BF16),
    ]
    out_specs = [
        pl.BlockSpec((ts, d), lambda i: (i, 0)),
        pl.BlockSpec((ts, d_pool), lambda i: (i, 0)),
        pl.BlockSpec((ts, d_q), lambda i: (i, 0)),
        pl.BlockSpec((ts, d_kv), lambda i: (i, 0)),
        pl.BlockSpec((ts, d_kv), lambda i: (i, 0)),
    ]
    if emit_kv_f32:
        out_shape += [jax.ShapeDtypeStruct((n, d_kv), F32)] * 2
        out_specs += [pl.BlockSpec((ts, d_kv), lambda i: (i, 0))] * 2

    return pl.pallas_call(
        functools.partial(_inproj_kernel, use_rope=use_rope, emit_kv_f32=emit_kv_f32, d_pool=d_pool),
        out_shape=out_shape,
        grid=(n // ts,),
        in_specs=in_specs,
        out_specs=out_specs,
        compiler_params=_cparams(("arbitrary",)),
        name="in_projection",
    )(*args)


def _attn_kernel(*refs, tk, kv_chunks, scale, n_cast, n_zero):
    n_kv = len(kv_chunks)
    q_ref = refs[0]
    kv_refs = refs[1:1 + 2 * n_kv]
    refs = refs[1 + 2 * n_kv:]
    cast_in = refs[:n_cast]
    o_ref = refs[n_cast]
    cast_out = refs[1 + n_cast:1 + 2 * n_cast]
    zero_out = refs[1 + 2 * n_cast:1 + 2 * n_cast + n_zero]
    for src, dst in zip(cast_in, cast_out):
        dst[...] = src[...].astype(dst.dtype)
    for dst in zero_out:
        dst[...] = jnp.zeros_like(dst)

    q = q_ref[0]
    tq = q.shape[0]
    qs = jnp.concatenate([q[:, j * HEAD_DIM:(j + 1) * HEAD_DIM] for j in range(Q_GROUP)], axis=0)
    rows = qs.shape[0]
    c2 = scale * 1.4426950408889634
    m = jnp.full((rows, 1), -jnp.inf, F32)
    l = jnp.zeros((rows, 1), F32)
    acc = jnp.zeros((rows, HEAD_DIM), F32)
    chunks = [(kv_refs[2 * i], kv_refs[2 * i + 1], c) for i, n in enumerate(kv_chunks) for c in range(n)]
    for k_ref, v_ref, c in chunks:
        kc = k_ref[0, c * tk:(c + 1) * tk, :]
        vc = v_ref[0, c * tk:(c + 1) * tk, :]
        s = lax.dot_general(qs, kc, (((1,), (1,)), ((), ())), preferred_element_type=F32) * c2
        m_new = jnp.maximum(m, jnp.max(s, axis=1, keepdims=True))
        alpha = jnp.exp2(m - m_new)
        p = jnp.exp2(s - m_new)
        l = alpha * l + jnp.sum(p, axis=1, keepdims=True)
        acc = alpha * acc + jnp.dot(p.astype(BF16), vc, preferred_element_type=F32)
        m = m_new
    out = acc / l
    o_ref[0] = jnp.concatenate([out[j * tq:(j + 1) * tq] for j in range(Q_GROUP)], axis=1).astype(o_ref.dtype)


def _side_rows(total_rows, row_bytes, n_steps):
    if total_rows % n_steps:
        return None
    rows = total_rows // n_steps
    if rows % BF16_SUBLANES or rows * row_bytes > SIDE_BLOCK_BYTES:
        return None
    return rows


def _attention(q3, kv_sources, cast_jobs=(), zero_jobs=()):
    b, s, d_q = q3.shape
    kv_lens = [k3.shape[1] for k3, _ in kv_sources]
    tq = _pick(s, (512, 256, 128))
    tk = next(t for t in (256, 128) if all(n % t == 0 for n in kv_lens))
    gw = Q_GROUP * HEAD_DIM
    nq = s // tq
    n_steps = b * N_KV_HEADS * nq

    def step(bi, g, qi):
        return ((bi * N_KV_HEADS + g) * nq + qi, 0)

    cast_rows = [_side_rows(a.shape[0], a.shape[1] * 4, n_steps) for a in cast_jobs]
    zero_rows = [_side_rows(r, c * 4, n_steps) for r, c in zero_jobs]
    fused_cast = [a for a, r in zip(cast_jobs, cast_rows) if r is not None]
    fused_zero = [z for z, r in zip(zero_jobs, zero_rows) if r is not None]

    in_specs = [pl.BlockSpec((1, tq, gw), lambda bi, g, qi: (bi, qi, g))]
    for n in kv_lens:
        in_specs += [pl.BlockSpec((1, n, HEAD_DIM), lambda bi, g, qi: (bi, 0, g))] * 2
    out_shape = [jax.ShapeDtypeStruct((b, s, d_q), BF16)]
    out_specs = [pl.BlockSpec((1, tq, gw), lambda bi, g, qi: (bi, qi, g))]
    for a, r in zip(cast_jobs, cast_rows):
        if r is not None:
            in_specs.append(pl.BlockSpec((r, a.shape[1]), step))
            out_shape.append(jax.ShapeDtypeStruct(a.shape, BF16))
            out_specs.append(pl.BlockSpec((r, a.shape[1]), step))
    for (rows, cols), r in zip(zero_jobs, zero_rows):
        if r is not None:
            out_shape.append(jax.ShapeDtypeStruct((rows, cols), F32))
            out_specs.append(pl.BlockSpec((r, cols), step))

    outs = pl.pallas_call(
        functools.partial(_attn_kernel, tk=tk, kv_chunks=tuple(n // tk for n in kv_lens),
                          scale=HEAD_DIM ** -0.5, n_cast=len(fused_cast), n_zero=len(fused_zero)),
        out_shape=out_shape,
        grid=(b, N_KV_HEADS, nq),
        in_specs=in_specs,
        out_specs=out_specs,
        compiler_params=_cparams(("arbitrary", "arbitrary", "arbitrary")),
        name="attention",
    )(q3, *[a for kv in kv_sources for a in kv], *fused_cast)

    it = iter(outs[1:])
    casts = [next(it) if r is not None else a.astype(BF16) for a, r in zip(cast_jobs, cast_rows)]
    zeros = [next(it) if r is not None else jnp.zeros(z, F32) for z, r in zip(zero_jobs, zero_rows)]
    return outs[0], casts, zeros


def _pool_kernel(up_ref, u_ref, un_ref, wp_ref, ps_ref, o_ref, ext_ref, *, tp, seq):
    i = pl.program_id(0)
    pos0 = (i * tp) % seq
    has_prev = pos0 > 0
    has_next = pos0 + tp < seq
    h = POOL_HALO
    ext_ref[0:h, :] = jnp.where(has_prev, up_ref[...], 0.0)
    ext_ref[h:h + tp, :] = u_ref[...]
    ext_ref[h + tp:h + tp + h, :] = jnp.where(has_next, un_ref[...], 0.0)

    gdim = wp_ref.shape[1]
    pos = pos0 + lax.broadcasted_iota(I32, (tp, 1), 0)
    for g, w in enumerate(POOL_WINDOWS):
        c0 = g * gdim
        acc = jnp.zeros((tp, gdim), F32)
        for j in range(-(w // 2), w - w // 2):
            acc = acc + ext_ref[h + j:h + j + tp, c0:c0 + gdim]
        lo = jnp.maximum(pos - w // 2, 0)
        hi = jnp.minimum(pos + (w - w // 2), seq)
        mean = acc / (hi - lo).astype(F32)
        pooled = (mean - u_ref[:, c0:c0 + gdim]).astype(BF16)
        y = jnp.dot(pooled, wp_ref[g], preferred_element_type=F32) * ps_ref[:, c0:c0 + gdim]
        o_ref[:, c0:c0 + gdim] = y.astype(o_ref.dtype)


def _pool_mixer(u2d, seq, w_pool_b, pool_scale):
    n, d_pool = u2d.shape
    tp = _pick(seq, (512, 256, 128))
    h = POOL_HALO
    assert max(POOL_WINDOWS) // 2 <= h
    nb = n // h
    per = tp // h
    return pl.pallas_call(
        functools.partial(_pool_kernel, tp=tp, seq=seq),
        out_shape=jax.ShapeDtypeStruct((n, d_pool), BF16),
        grid=(n // tp,),
        in_specs=[
            pl.BlockSpec((h, d_pool), lambda i: (jnp.maximum(i * per - 1, 0), 0)),
            pl.BlockSpec((tp, d_pool), lambda i: (i, 0)),
            pl.BlockSpec((h, d_pool), lambda i: (jnp.minimum((i + 1) * per, nb - 1), 0)),
            _resident(w_pool_b.shape, lambda i: (0, 0, 0)),
            _resident((1, d_pool), lambda i: (0, 0)),
        ],
        out_specs=pl.BlockSpec((tp, d_pool), lambda i: (i, 0)),
        scratch_shapes=[pltpu.VMEM((tp + 2 * h, d_pool), F32)],
        compiler_params=_cparams(("arbitrary",)),
        name="pool_mixer",
    )(u2d, u2d, u2d, w_pool_b, pool_scale.reshape(1, d_pool))


def _merge_kernel(h_ref, pm_ref, at_ref, wga_ref, wgb_ref, bga_ref, bgb_ref, wa_ref, wb_ref, o_ref):
    hb = h_ref[...]
    ga = jax.nn.sigmoid(jnp.dot(hb, wga_ref[...], preferred_element_type=F32) + bga_ref[...])
    gb = jax.nn.sigmoid(jnp.dot(hb, wgb_ref[...], preferred_element_type=F32) + bgb_ref[...])
    ya = jnp.dot(pm_ref[...], wa_ref[...], preferred_element_type=F32)
    yb = jnp.dot(at_ref[...], wb_ref[...], preferred_element_type=F32)
    o_ref[...] = (ga * ya + gb * yb).astype(o_ref.dtype)


def _gated_merge(h2d, pm2d, at2d, w_gate_b, b_gate, w_a_b, w_b_b):
    n, d = h2d.shape
    d_pool = pm2d.shape[1]
    d_q = at2d.shape[1]
    tm = _pick(n, (1024, 512, 256, 128))
    tn = _pick(d, (512, 256, 128))
    nj = d // tn
    bg = b_gate.reshape(1, 2 * d)
    return pl.pallas_call(
        _merge_kernel,
        out_shape=jax.ShapeDtypeStruct((n, d), BF16),
        grid=(n // tm, nj),
        in_specs=[
            pl.BlockSpec((tm, d), lambda i, j: (i, 0)),
            pl.BlockSpec((tm, d_pool), lambda i, j: (i, 0)),
            pl.BlockSpec((tm, d_q), lambda i, j: (i, 0)),
            pl.BlockSpec((d, tn), lambda i, j: (0, j)),
            pl.BlockSpec((d, tn), lambda i, j: (0, nj + j)),
            pl.BlockSpec((1, tn), lambda i, j: (0, j)),
            pl.BlockSpec((1, tn), lambda i, j: (0, nj + j)),
            pl.BlockSpec((d_pool, tn), lambda i, j: (0, j)),
            pl.BlockSpec((d_q, tn), lambda i, j: (0, j)),
        ],
        out_specs=pl.BlockSpec((tm, tn), lambda i, j: (i, j)),
        compiler_params=_cparams(("arbitrary", "arbitrary")),
        name="gated_merge",
    )(h2d, pm2d, at2d, w_gate_b, w_gate_b, bg, bg, w_a_b, w_b_b)


def _outproj_kernel(mg_ref, x_ref, mod_ref, wo_ref, g2_ref, wr_ref, br_ref, x1_ref, h2_ref, lg_ref):
    mix = jnp.dot(mg_ref[...], wo_ref[...], preferred_element_type=F32)
    gate1 = mod_ref[0, 2:3, :]
    shift2 = mod_ref[0, 3:4, :]
    scale2 = mod_ref[0, 4:5, :]
    x1 = x_ref[...] + gate1 * mix
    x1_ref[...] = x1
    ms = jnp.mean(x1 * x1, axis=-1, keepdims=True)
    h2 = (x1 * lax.rsqrt(ms + EPS) * g2_ref[...]) * (1.0 + scale2) + shift2
    h2_ref[...] = h2
    lg_ref[...] = jnp.dot(h2.astype(BF16), wr_ref[...], preferred_element_type=F32) + br_ref[...]


def _out_projection(mg2d, x2d, mod3, mod_row, w_o_b, g2, w_router, b_router):
    n, d = x2d.shape
    ne = w_router.shape[1]
    tm = _pick(n, (256, 128))
    return pl.pallas_call(
        _outproj_kernel,
        out_shape=[
            jax.ShapeDtypeStruct((n, d), F32),
            jax.ShapeDtypeStruct((n, d), F32),
            jax.ShapeDtypeStruct((n, ne), F32),
        ],
        grid=(n // tm,),
        in_specs=[
            pl.BlockSpec((tm, d), lambda i: (i, 0)),
            pl.BlockSpec((tm, d), lambda i: (i, 0)),
            pl.BlockSpec((1, N_MOD, d), lambda i: (mod_row(i * tm), 0, 0)),
            _resident((d, d), lambda i: (0, 0)),
            _resident((1, d), lambda i: (0, 0)),
            _resident((d, ne), lambda i: (0, 0)),
            _resident((1, ne), lambda i: (0, 0)),
        ],
        out_specs=[
            pl.BlockSpec((tm, d), lambda i: (i, 0)),
            pl.BlockSpec((tm, d), lambda i: (i, 0)),
            pl.BlockSpec((tm, ne), lambda i: (i, 0)),
        ],
        compiler_params=_cparams(("arbitrary",)),
        name="out_projection",
    )(mg2d, x2d, mod3, w_o_b, g2.reshape(1, d), w_router, b_router.reshape(1, ne))


def _route_kernel(lg_ref, idx_ref, gw_ref, rk_ref, cnt_ref):
    step = pl.program_id(0)

    @pl.when(step == 0)
    def _():
        cnt_ref[...] = jnp.zeros_like(cnt_ref)

    lg = lg_ref[...]
    tr, ne = lg.shape
    lane = lax.broadcasted_iota(I32, (tr, ne), 1)
    work = lg
    vals, idxs = [], []
    member = jnp.zeros((tr, ne), F32)
    for _ in range(TOP_K):
        mx = jnp.max(work, axis=-1, keepdims=True)
        ix = jnp.min(jnp.where(work == mx, lane, ne), axis=-1, keepdims=True)
        sel = lane == ix
        vals.append(mx)
        idxs.append(ix)
        member = jnp.where(sel, 1.0, member)
        work = jnp.where(sel, -jnp.inf, work)

    ex = [jnp.exp(v - vals[0]) for v in vals]
    den = ex[0]
    for e in ex[1:]:
        den = den + e

    row = lax.broadcasted_iota(I32, (tr, tr), 0)
    col = lax.broadcasted_iota(I32, (tr, tr), 1)
    lower = jnp.where(col < row, 1.0, 0.0).astype(BF16)
    before = jnp.dot(lower, member.astype(BF16), preferred_element_type=F32) + cnt_ref[...]

    k_lane = lax.broadcasted_iota(I32, (tr, TOP_K), 1)
    idx_out = jnp.zeros((tr, TOP_K), I32)
    gw_out = jnp.zeros((tr, TOP_K), F32)
    rk_out = jnp.zeros((tr, TOP_K), I32)
    for k in range(TOP_K):
        rk = jnp.sum(jnp.where(lane == idxs[k], before, 0.0), axis=-1, keepdims=True).astype(I32)
        idx_out = jnp.where(k_lane == k, idxs[k], idx_out)
        gw_out = jnp.where(k_lane == k, ex[k] / den, gw_out)
        rk_out = jnp.where(k_lane == k, rk, rk_out)
    idx_ref[...] = idx_out
    gw_ref[...] = gw_out
    rk_ref[...] = rk_out
    cnt_ref[...] += jnp.sum(member, axis=0, keepdims=True)


def _routing(logits):
    n, ne = logits.shape
    tr = _pick(n, (512, 256, 128))
    return pl.pallas_call(
        _route_kernel,
        out_shape=[
            jax.ShapeDtypeStruct((n, TOP_K), I32),
            jax.ShapeDtypeStruct((n, TOP_K), F32),
            jax.ShapeDtypeStruct((n, TOP_K), I32),
            jax.ShapeDtypeStruct((1, ne), F32),
        ],
        grid=(n // tr,),
        in_specs=[pl.BlockSpec((tr, ne), lambda i: (i, 0))],
        out_specs=[
            pl.BlockSpec((tr, TOP_K), lambda i: (i, 0)),
            pl.BlockSpec((tr, TOP_K), lambda i: (i, 0)),
            pl.BlockSpec((tr, TOP_K), lambda i: (i, 0)),
            pl.BlockSpec((1, ne), lambda i: (0, 0)),
        ],
        compiler_params=_cparams(("arbitrary",)),
        name="routing",
    )(logits)


def _dispatch_kernel(dest_ref, h_ref, xs_in_hbm, xs_hbm, sem, *, td):
    del xs_in_hbm

    def start(r, carry):
        for k in range(TOP_K):
            d = dest_ref[0, 0, r * TOP_K + k]
            pltpu.make_async_copy(h_ref.at[pl.ds(r, 1)], xs_hbm.at[pl.ds(d, 1)], sem).start()
        return carry

    lax.fori_loop(0, td, start, 0)
    for _ in range(TOP_K):
        pltpu.make_async_copy(h_ref, xs_hbm.at[pl.ds(0, td)], sem).wait()


def _dispatch(h2d, dest, row0, xs0):
    n, d = h2d.shape
    n_slots = xs0.shape[0]
    td = _pick(math.gcd(n, row0) if row0 else n, (512, 256, 128))
    assert dest.shape[0] % td == 0
    dest3 = dest.reshape(dest.shape[0] // td, 1, td * TOP_K)
    blk0 = row0 // td
    return pl.pallas_call(
        functools.partial(_dispatch_kernel, td=td),
        out_shape=jax.ShapeDtypeStruct((n_slots, d), h2d.dtype),
        grid=(n // td,),
        in_specs=[
            pl.BlockSpec((1, 1, td * TOP_K), lambda i: (blk0 + i, 0, 0), memory_space=pltpu.SMEM),
            pl.BlockSpec((td, d), lambda i: (i, 0)),
            pl.BlockSpec(memory_space=pl.ANY),
        ],
        out_specs=pl.BlockSpec(memory_space=pl.ANY),
        scratch_shapes=[pltpu.SemaphoreType.DMA(())],
        input_output_aliases={2: 0},
        compiler_params=_cparams(("arbitrary",)),
        name="dispatch",
    )(dest3, h2d, xs0)


def _expert_kernel(te_ref, tr_ref, nu_ref, x_ref, wg_ref, wl_ref, bg_ref, bl_ref, wd_ref, bd_ref, y_ref, *, nf):
    del te_ref
    i = pl.program_id(0)
    f = pl.program_id(1)
    used = i < nu_ref[0]
    rows = tr_ref[jnp.minimum(i, nu_ref[0] - 1)]
    te = x_ref.shape[0]
    half = te // 2

    def mlp(m, first):
        xb = x_ref[0:m, :].astype(BF16)
        g = jnp.dot(xb, wg_ref[0], preferred_element_type=F32) + bg_ref[0]
        lin = jnp.dot(xb, wl_ref[0], preferred_element_type=F32) + bl_ref[0]
        g = jnp.minimum(g, SWIGLU_LIMIT)
        lin = jnp.clip(lin, -SWIGLU_LIMIT, SWIGLU_LIMIT)
        act = g * jax.nn.sigmoid(SWIGLU_ALPHA * g) * (lin + 1.0)
        down = jnp.dot(act.astype(BF16), wd_ref[0], preferred_element_type=F32)
        if first:
            y_ref[0:m, :] = down + bd_ref[0]
            if m < te:
                y_ref[m:te, :] = jnp.zeros((te - m, y_ref.shape[1]), F32)
        else:
            y_ref[0:m, :] += down

    for m, occupied in ((te, rows > half), (half, rows <= half)):
        for first in (True, False):
            at_step = (f == 0) if first else (f != 0)

            @pl.when(jnp.logical_and(jnp.logical_and(used, occupied), at_step))
            def _():
                mlp(m, first)

    @pl.when(jnp.logical_and(jnp.logical_not(used), f == nf - 1))
    def _():
        y_ref[...] = jnp.zeros_like(y_ref)


def _expert_mlp(xs, tile_expert, tile_rows, n_used, w_gu_b, b_gu, w_down_b, b_down, te):
    n_slots, d = xs.shape
    ne, _, d_ff2 = w_gu_b.shape
    d_ff = d_ff2 // 2
    tf = _pick(d_ff, (1024, 512, 256, 128))
    nf = d_ff // tf
    n_tiles = n_slots // te

    def tile(i, nu):
        return jnp.minimum(i, nu[0] - 1)

    def ff(i, f, nu):
        return jnp.where(i < nu[0], f, nf - 1)

    grid_spec = pltpu.PrefetchScalarGridSpec(
        num_scalar_prefetch=3,
        grid=(n_tiles, nf),
        in_specs=[
            pl.BlockSpec((te, d), lambda i, f, tx, tr, nu: (tile(i, nu), 0)),
            pl.BlockSpec((1, d, tf), lambda i, f, tx, tr, nu: (tx[tile(i, nu)], 0, ff(i, f, nu))),
            pl.BlockSpec((1, d, tf), lambda i, f, tx, tr, nu: (tx[tile(i, nu)], 0, nf + ff(i, f, nu))),
            pl.BlockSpec((1, 1, tf), lambda i, f, tx, tr, nu: (tx[tile(i, nu)], 0, ff(i, f, nu))),
            pl.BlockSpec((1, 1, tf), lambda i, f, tx, tr, nu: (tx[tile(i, nu)], 0, nf + ff(i, f, nu))),
            pl.BlockSpec((1, tf, d), lambda i, f, tx, tr, nu: (tx[tile(i, nu)], ff(i, f, nu), 0)),
            pl.BlockSpec((1, 1, d), lambda i, f, tx, tr, nu: (tx[tile(i, nu)], 0, 0)),
        ],
        out_specs=pl.BlockSpec((te, d), lambda i, f, tx, tr, nu: (i, 0)),
    )
    return pl.pallas_call(
        functools.partial(_expert_kernel, nf=nf),
        out_shape=jax.ShapeDtypeStruct((n_slots, d), F32),
        grid_spec=grid_spec,
        compiler_params=_cparams(("arbitrary", "arbitrary")),
        name="expert_mlp",
    )(tile_expert, tile_rows, n_used, xs, w_gu_b, w_gu_b, b_gu.reshape(ne, 1, d_ff2), b_gu.reshape(ne, 1, d_ff2),
      w_down_b, b_down.reshape(ne, 1, d))


def _combine_kernel(dcur_ref, dnext_ref, x1_ref, gw_ref, mod_ref, gf_ref, y_hbm, o_ref, ybuf, sems, *,
                    tc, n_steps):
    i = pl.program_id(0)
    slot = i % 2

    def issue(dest_ref, s):
        def start(r, carry):
            for k in range(TOP_K):
                d = dest_ref[0, 0, r * TOP_K + k]
                pltpu.make_async_copy(y_hbm.at[pl.ds(d, 1)], ybuf.at[s, k, pl.ds(r, 1)], sems.at[s]).start()
            return carry

        lax.fori_loop(0, tc, start, 0)

    @pl.when(i == 0)
    def _():
        issue(dcur_ref, 0)

    for s in range(2):
        @pl.when(jnp.logical_and(i + 1 < n_steps, slot == 1 - s))
        def _():
            issue(dnext_ref, s)

    for k in range(TOP_K):
        pltpu.make_async_copy(y_hbm.at[pl.ds(0, tc)], ybuf.at[slot, k], sems.at[slot]).wait()

    gw = gw_ref[...]
    moe = gw[:, 0:1] * ybuf[slot, 0]
    for k in range(1, TOP_K):
        moe = moe + gw[:, k:k + 1] * ybuf[slot, k]
    gate2 = mod_ref[0, 5:6, :]
    x2 = x1_ref[...] + gate2 * moe
    ms = jnp.mean(x2 * x2, axis=-1, keepdims=True)
    o_ref[...] = x2 * lax.rsqrt(ms + EPS) * gf_ref[...]


def _combine(x1, gate_w, dest, row0, y, mod3, mod_row, final_g):
    n, d = x1.shape
    tc = _pick(math.gcd(n, row0) if row0 else n, (256, 128))
    assert dest.shape[0] % tc == 0
    n_steps = n // tc
    dest3 = dest.reshape(dest.shape[0] // tc, 1, tc * TOP_K)
    blk0 = row0 // tc
    return pl.pallas_call(
        functools.partial(_combine_kernel, tc=tc, n_steps=n_steps),
        out_shape=jax.ShapeDtypeStruct((n, d), F32),
        grid=(n_steps,),
        in_specs=[
            pl.BlockSpec((1, 1, tc * TOP_K), lambda i: (blk0 + i, 0, 0), memory_space=pltpu.SMEM),
            pl.BlockSpec((1, 1, tc * TOP_K), lambda i: (blk0 + jnp.minimum(i + 1, n_steps - 1), 0, 0),
                         memory_space=pltpu.SMEM),
            pl.BlockSpec((tc, d), lambda i: (i, 0)),
            pl.BlockSpec((tc, TOP_K), lambda i: (blk0 + i, 0)),
            pl.BlockSpec((1, N_MOD, d), lambda i: (mod_row(i * tc), 0, 0)),
            _resident((1, d), lambda i: (0, 0)),
            pl.BlockSpec(memory_space=pl.ANY),
        ],
        out_specs=pl.BlockSpec((tc, d), lambda i: (i, 0)),
        scratch_shapes=[pltpu.VMEM((2, TOP_K, tc, d), F32), pltpu.SemaphoreType.DMA((2,))],
        compiler_params=_cparams(("arbitrary",)),
        name="combine",
    )(dest3, dest3, x1, gate_w, mod3, final_g.reshape(1, d), y)


def _rope_tables(seq):
    rows = seq // GRID_W
    row = jnp.repeat(jnp.arange(rows), GRID_W).astype(F32)
    col = jnp.tile(jnp.arange(GRID_W), rows).astype(F32)
    freqs = ROPE_THETA ** (-jnp.arange(ROPE_PAIRS_PER_AXIS, dtype=F32) / ROPE_PAIRS_PER_AXIS)
    ang = jnp.concatenate([row[:, None] * freqs, col[:, None] * freqs], axis=-1)
    cos, sin = jnp.cos(ang), jnp.sin(ang)
    return jnp.concatenate([cos, cos], axis=-1), jnp.concatenate([-sin, sin], axis=-1)


def _slot_tiles(n_tokens, n_experts):
    return (n_tokens * TOP_K) // EXPERT_TILE_ROWS + n_experts


def _moe(groups, wts, final_g, mod3, xs0):
    te = EXPERT_TILE_ROWS
    sizes = [g[0].shape[0] for g in groups]
    n = sum(sizes)
    logits = jnp.concatenate([g[2] for g in groups], axis=0)
    ne = logits.shape[1]
    top_i, gate_w, rank, counts = _routing(logits)

    counts = counts.reshape(ne).astype(I32)
    padded = (counts + te - 1) // te * te
    pad_end = jnp.cumsum(padded)
    pad_start = pad_end - padded
    n_tiles = _slot_tiles(n, ne)
    assert xs0.shape[0] == n_tiles * te
    tile_start = jnp.arange(n_tiles, dtype=I32) * te
    tile_expert = jnp.minimum(jnp.sum((pad_end[None, :] <= tile_start[:, None]).astype(I32), axis=1), ne - 1)
    tile_rows = jnp.clip(counts[tile_expert] - (tile_start - pad_start[tile_expert]), 0, te).astype(I32)
    n_used = (pad_end[-1:] // te).astype(I32)
    onehot = top_i[:, :, None] == jnp.arange(ne, dtype=I32)
    dest = jnp.sum(jnp.where(onehot, pad_start, 0), axis=-1) + rank

    offs = [sum(sizes[:i]) for i in range(len(sizes))]
    xs = xs0
    for (_, h2, _, _), o in zip(groups, offs):
        xs = _dispatch(h2, dest, o, xs)
    y = _expert_mlp(xs, tile_expert, tile_rows, n_used, wts["w_gu"], wts["b_gu"], wts["w_down"], wts["b_down"], te)
    return [_combine(x1, gate_w, dest, o, y, mod3, mod_row, final_g)
            for (x1, _, _, mod_row), o in zip(groups, offs)]


def _mixer(x2d, batch, seq, mod3, mod_row, rope, kv_ctx, wts, emit_kv_f32, cast_jobs=(), zero_jobs=()):
    outs = _in_projection(x2d, seq, mod3, mod_row, wts["norm1_g"], wts["w_in"], wts["q_norm_g"],
                          wts["k_norm_g"], rope, emit_kv_f32)
    h, u, q, k, v = outs[:5]
    d_q = q.shape[1]
    d_kv = k.shape[1]
    q3 = q.reshape(batch, seq, d_q)
    k3 = k.reshape(batch, seq, d_kv)
    v3 = v.reshape(batch, seq, d_kv)
    kv_sources = ([kv_ctx] if kv_ctx is not None else []) + [(k3, v3)]
    attn, casts, zeros = _attention(q3, kv_sources, cast_jobs, zero_jobs)
    attn = attn.reshape(batch * seq, d_q)
    pm = _pool_mixer(u, seq, wts["w_pool"], wts["pool_scale"])
    merged = _gated_merge(h, pm, attn, wts["w_gate"], wts["b_gate"], wts["w_a_out"], wts["w_b_out"])
    x1, h2, logits = _out_projection(merged, x2d, mod3, mod_row, wts["w_o"], wts["norm2_g"],
                                     wts["w_router"], wts["b_router"])
    return (x1, h2, logits, mod_row), outs[5:], casts, zeros


def kernel(x_prompt, x_sample, cache_k, cache_v, c, c_ctx, w_mod, b_mod, norm1_g, w_in, q_norm_g, k_norm_g, w_pool, pool_scale, w_a_out, w_b_out, w_gate, b_gate, w_o, norm2_g, w_router, b_router, w_gu, b_gu, w_down, b_down, final_g):
    depth = w_mod.shape[0]
    assert depth == 1, "the final norm is fused into the last combine; one trunk layer is supported"
    bp, sp, d = x_prompt.shape
    bs, ss, _ = x_sample.shape
    l = 0

    n_rows = -(-(bs + 1) // SUBLANES) * SUBLANES
    cvec = jnp.zeros((n_rows, d), F32).at[:bs].set(c).at[bs].set(c_ctx)
    mod3 = _modulation(cvec, w_mod[l], b_mod[l]).reshape(n_rows, N_MOD, d)

    wts = dict(
        norm1_g=norm1_g[l], w_in=w_in[l].astype(BF16), q_norm_g=q_norm_g[l], k_norm_g=k_norm_g[l],
        w_pool=w_pool[l].astype(BF16), pool_scale=pool_scale[l], w_a_out=w_a_out[l].astype(BF16),
        w_b_out=w_b_out[l].astype(BF16), w_gate=w_gate[l].astype(BF16), b_gate=b_gate[l],
        w_o=w_o[l].astype(BF16), norm2_g=norm2_g[l], w_router=w_router[l].astype(BF16), b_router=b_router[l],
        b_gu=b_gu[l], b_down=b_down[l],
    )

    gp, (kf, vf), _, _ = _mixer(x_prompt.reshape(bp * sp, d), bp, sp, mod3, lambda t: bs, None, None,
                                wts, True)
    kv_ctx = (cache_k[:, l].reshape(bs, -1, N_KV_HEADS * HEAD_DIM).astype(BF16),
              cache_v[:, l].reshape(bs, -1, N_KV_HEADS * HEAD_DIM).astype(BF16))
    ne, _, d_ff2 = w_gu[l].shape
    n_slots = _slot_tiles(bp * sp + bs * ss, ne) * EXPERT_TILE_ROWS
    gs, _, (w_gu_b, w_down_b), (xs0,) = _mixer(
        x_sample.reshape(bs * ss, d), bs, ss, mod3, lambda t: t // ss, _rope_tables(ss), kv_ctx, wts, False,
        cast_jobs=(w_gu[l].reshape(ne * d, d_ff2), w_down[l].reshape(-1, d)), zero_jobs=((n_slots, d),))
    wts["w_gu"] = w_gu_b.reshape(ne, d, d_ff2)
    wts["w_down"] = w_down_b.reshape(w_down[l].shape)
    yp, ys = _moe([gp, gs], wts, final_g, mod3, xs0)

    y_prompt = yp.reshape(bp, sp, d)
    y_sample = ys.reshape(bs, ss, d)
    new_cache_k = kf.reshape(bp, 1, sp, N_KV_HEADS, HEAD_DIM)
    new_cache_v = vf.reshape(bp, 1, sp, N_KV_HEADS, HEAD_DIM)
    return (y_prompt, y_sample, new_cache_k, new_cache_v)
```

```python
import functools
import math

import jax
import jax.numpy as jnp
from jax import lax
from jax.experimental import pallas as pl
from jax.experimental.pallas import tpu as pltpu

F32 = jnp.float32
BF16 = jnp.bfloat16
I32 = jnp.int32

GRID_W = 64
N_HEADS = 16
N_KV_HEADS = 4
HEAD_DIM = 128
Q_GROUP = N_HEADS // N_KV_HEADS
ROPE_PAIRS_PER_AXIS = HEAD_DIM // 4
ROPE_THETA = 10000.0
POOL_WINDOWS = (2, 4, 8, 16)
TOP_K = 4
SWIGLU_LIMIT = 7.0
SWIGLU_ALPHA = 1.702
N_MOD = 6
EPS = 1e-6

LANES = 128
SUBLANES = 8
BF16_SUBLANES = 16
VMEM_LIMIT_BYTES = 56 * 1024 * 1024
SIDE_BLOCK_BYTES = 6 * 1024 * 1024

OUT_PROJ_TILES = (512, 256, 128)
OUT_PROJ_SUB_BLOCKS = 2
IN_PROJ_SUB_BLOCKS = 2
EXPERT_TILE_ROWS = 512
POOL_HALO = SUBLANES


def _cparams(semantics):
    return pltpu.CompilerParams(dimension_semantics=semantics, vmem_limit_bytes=VMEM_LIMIT_BYTES)


def _pick(n, candidates):
    for c in candidates:
        if n % c == 0:
            return c
    raise ValueError(f"no tile in {candidates} divides {n}")


def _resident(shape, index_map):
    return pl.BlockSpec(shape, index_map, pipeline_mode=pl.Buffered(1))


def _mod_kernel(c_ref, w_ref, b_ref, o_ref):
    c = c_ref[...]
    s = (c * jax.nn.sigmoid(c)).astype(BF16)
    o_ref[...] = jnp.dot(s, w_ref[...].astype(BF16), preferred_element_type=F32) + b_ref[...]


def _modulation(cvec, w_mod, b_mod):
    r, d = cvec.shape
    n6 = w_mod.shape[1]
    tn = _pick(n6, (1024, 512, 256, 128))
    return pl.pallas_call(
        _mod_kernel,
        out_shape=jax.ShapeDtypeStruct((r, n6), F32),
        grid=(n6 // tn,),
        in_specs=[
            pl.BlockSpec((r, d), lambda j: (0, 0)),
            pl.BlockSpec((d, tn), lambda j: (0, j)),
            pl.BlockSpec((1, tn), lambda j: (0, j)),
        ],
        out_specs=pl.BlockSpec((r, tn), lambda j: (0, j)),
        compiler_params=_cparams(("arbitrary",)),
        name="modulation",
    )(cvec, w_mod, b_mod.reshape(1, n6))


def _inproj_kernel(*refs, use_rope, emit_kv_f32, d_pool):
    row_tiled = [True] + [False] * 5 + [True] * (len(refs) - 6)
    rows = refs[0].shape[0] // IN_PROJ_SUB_BLOCKS
    for sb in range(IN_PROJ_SUB_BLOCKS):
        views = [r.at[pl.ds(sb * rows, rows)] if t else r for r, t in zip(refs, row_tiled)]
        _inproj_rows(*views, use_rope=use_rope, emit_kv_f32=emit_kv_f32, d_pool=d_pool)


def _inproj_rows(*refs, use_rope, emit_kv_f32, d_pool):
    it = iter(refs)
    x_ref, mod_ref, g1_ref, w_ref, qg_ref, kg_ref = (next(it) for _ in range(6))
    cos_ref = sin_ref = None
    if use_rope:
        cos_ref, sin_ref = next(it), next(it)
    h_ref, u_ref, q_ref, k_ref, v_ref = (next(it) for _ in range(5))
    kf_ref = vf_ref = None
    if emit_kv_f32:
        kf_ref, vf_ref = next(it), next(it)

    x = x_ref[...]
    ms = jnp.mean(x * x, axis=-1, keepdims=True)
    xn = x * lax.rsqrt(ms + EPS) * g1_ref[...]
    shift = mod_ref[0, 0:1, :]
    scale = mod_ref[0, 1:2, :]
    hb = (xn * (1.0 + scale) + shift).astype(BF16)
    h_ref[...] = hb

    d_q = N_HEADS * HEAD_DIM
    d_kv = N_KV_HEADS * HEAD_DIM
    chunk = Q_GROUP * HEAD_DIM

    for c0 in range(0, d_pool, chunk):
        u_ref[:, c0:c0 + chunk] = jnp.dot(hb, w_ref[:, c0:c0 + chunk], preferred_element_type=F32)

    if use_rope:
        cos = cos_ref[...]
        sin = sin_ref[...]

    def head(xh, g):
        hm = jnp.mean(xh * xh, axis=-1, keepdims=True)
        y = xh * lax.rsqrt(hm + EPS) * g
        if use_rope:
            y = y * cos + pltpu.roll(y, HEAD_DIM // 2, axis=1) * sin
        return y

    qg = qg_ref[...]
    kg = kg_ref[...]
    for c0 in range(0, d_q, chunk):
        pr = jnp.dot(hb, w_ref[:, d_pool + c0:d_pool + c0 + chunk], preferred_element_type=F32)
        for j in range(chunk // HEAD_DIM):
            yh = head(pr[:, j * HEAD_DIM:(j + 1) * HEAD_DIM], qg)
            q_ref[:, c0 + j * HEAD_DIM:c0 + (j + 1) * HEAD_DIM] = yh.astype(BF16)

    pk = jnp.dot(hb, w_ref[:, d_pool + d_q:d_pool + d_q + d_kv], preferred_element_type=F32)
    for j in range(N_KV_HEADS):
        yh = head(pk[:, j * HEAD_DIM:(j + 1) * HEAD_DIM], kg)
        k_ref[:, j * HEAD_DIM:(j + 1) * HEAD_DIM] = yh.astype(BF16)
        if emit_kv_f32:
            kf_ref[:, j * HEAD_DIM:(j + 1) * HEAD_DIM] = yh

    pv = jnp.dot(hb, w_ref[:, d_pool + d_q + d_kv:], preferred_element_type=F32)
    v_ref[...] = pv.astype(BF16)
    if emit_kv_f32:
        vf_ref[...] = pv


def _in_projection(x2d, seq, mod3, mod_row, g1, w_in_b, qg, kg, rope, emit_kv_f32):
    n, d = x2d.shape
    d_in = w_in_b.shape[1]
    d_q = N_HEADS * HEAD_DIM
    d_kv = N_KV_HEADS * HEAD_DIM
    d_pool = d_in - d_q - 2 * d_kv
    ts = _pick(seq, (512, 256, 128))
    tiles_per_seq = seq // ts
    use_rope = rope is not None

    in_specs = [
        pl.BlockSpec((ts, d), lambda i: (i, 0)),
        pl.BlockSpec((1, N_MOD, d), lambda i: (mod_row(i * ts), 0, 0)),
        _resident((1, d), lambda i: (0, 0)),
        _resident((d, d_in), lambda i: (0, 0)),
        _resident((1, HEAD_DIM), lambda i: (0, 0)),
        _resident((1, HEAD_DIM), lambda i: (0, 0)),
    ]
    args = [x2d, mod3, g1.reshape(1, d), w_in_b, qg.reshape(1, HEAD_DIM), kg.reshape(1, HEAD_DIM)]
    if use_rope:
        in_specs += [pl.BlockSpec((ts, HEAD_DIM), lambda i: (i % tiles_per_seq, 0))] * 2
        args += list(rope)

    out_shape = [
        jax.ShapeDtypeStruct((n, d), BF16),
        jax.ShapeDtypeStruct((n, d_pool), F32),
        jax.ShapeDtypeStruct((n, d_q), BF16),
        jax.ShapeDtypeStruct((n, d_kv), BF16),
        jax.ShapeDtypeStruct((n, d_kv), BF16),
    ]
    out_specs = [
        pl.BlockSpec((ts, d), lambda i: (i, 0)),
        pl.BlockSpec((ts, d_pool), lambda i: (i, 0)),
        pl.BlockSpec((ts, d_q), lambda i: (i, 0)),
        pl.BlockSpec((ts, d_kv), lambda i: (i, 0)),
        pl.BlockSpec((ts, d_kv), lambda i: (i, 0)),
    ]
    if emit_kv_f32:
        out_shape += [jax.ShapeDtypeStruct((n, d_kv), F32)] * 2
        out_specs += [pl.BlockSpec((ts, d_kv), lambda i: (i, 0))] * 2

    return pl.pallas_call(
        functools.partial(_inproj_kernel, use_rope=use_rope, emit_kv_f32=emit_kv_f32, d_pool=d_pool),
        out_shape=out_shape,
        grid=(n // ts,),
        in_specs=in_specs,
        out_specs=out_specs,
        compiler_params=_cparams(("arbitrary",)),
        name="in_projection",
    )(*args)


def _attn_kernel(*refs, tk, kv_chunks, scale, n_cast, n_zero):
    n_kv = len(kv_chunks)
    q_ref = refs[0]
    kv_refs = refs[1:1 + 2 * n_kv]
    refs = refs[1 + 2 * n_kv:]
    cast_in = refs[:n_cast]
    o_ref = refs[n_cast]
    cast_out = refs[1 + n_cast:1 + 2 * n_cast]
    zero_out = refs[1 + 2 * n_cast:1 + 2 * n_cast + n_zero]
    for src, dst in zip(cast_in, cast_out):
        dst[...] = src[...].astype(dst.dtype)
    for dst in zero_out:
        dst[...] = jnp.zeros_like(dst)

    q = q_ref[0]
    tq = q.shape[0]
    qs = jnp.concatenate([q[:, j * HEAD_DIM:(j + 1) * HEAD_DIM] for j in range(Q_GROUP)], axis=0)
    rows = qs.shape[0]
    c2 = scale * 1.4426950408889634
    m = jnp.full((rows, 1), -jnp.inf, F32)
    l = jnp.zeros((rows, 1), F32)
    acc = jnp.zeros((rows, HEAD_DIM), F32)
    chunks = [(kv_refs[2 * i], kv_refs[2 * i + 1], c) for i, n in enumerate(kv_chunks) for c in range(n)]
    for k_ref, v_ref, c in chunks:
        kc = k_ref[0, c * tk:(c + 1) * tk, :]
        vc = v_ref[0, c * tk:(c + 1) * tk, :]
        s = lax.dot_general(qs, kc, (((1,), (1,)), ((), ())), preferred_element_type=F32) * c2
        m_new = jnp.maximum(m, jnp.max(s, axis=1, keepdims=True))
        alpha = jnp.exp2(m - m_new)
        p = jnp.exp2(s - m_new)
        l = alpha * l + jnp.sum(p, axis=1, keepdims=True)
        acc = alpha * acc + jnp.dot(p.astype(BF16), vc, preferred_element_type=F32)
        m = m_new
    out = acc / l
    o_ref[0] = jnp.concatenate([out[j * tq:(j + 1) * tq] for j in range(Q_GROUP)], axis=1).astype(o_ref.dtype)


def _side_rows(total_rows, row_bytes, n_steps):
    if total_rows % n_steps:
        return None
    rows = total_rows // n_steps
    if rows % BF16_SUBLANES or rows * row_bytes > SIDE_BLOCK_BYTES:
        return None
    return rows


def _attention(q3, kv_sources, cast_jobs=(), zero_jobs=()):
    b, s, d_q = q3.shape
    kv_lens = [k3.shape[1] for k3, _ in kv_sources]
    tq = _pick(s, (512, 256, 128))
    tk = next(t for t in (256, 128) if all(n % t == 0 for n in kv_lens))
    gw = Q_GROUP * HEAD_DIM
    nq = s // tq
    n_steps = b * N_KV_HEADS * nq

    def step(bi, g, qi):
        return ((bi * N_KV_HEADS + g) * nq + qi, 0)

    cast_rows = [_side_rows(a.shape[0], a.shape[1] * 4, n_steps) for a in cast_jobs]
    zero_rows = [_side_rows(r, c * 4, n_steps) for r, c in zero_jobs]
    fused_cast = [a for a, r in zip(cast_jobs, cast_rows) if r is not None]
    fused_zero = [z for z, r in zip(zero_jobs, zero_rows) if r is not None]

    in_specs = [pl.BlockSpec((1, tq, gw), lambda bi, g, qi: (bi, qi, g))]
    for n in kv_lens:
        in_specs += [pl.BlockSpec((1, n, HEAD_DIM), lambda bi, g, qi: (bi, 0, g))] * 2
    out_shape = [jax.ShapeDtypeStruct((b, s, d_q), BF16)]
    out_specs = [pl.BlockSpec((1, tq, gw), lambda bi, g, qi: (bi, qi, g))]
    for a, r in zip(cast_jobs, cast_rows):
        if r is not None:
            in_specs.append(pl.BlockSpec((r, a.shape[1]), step))
            out_shape.append(jax.ShapeDtypeStruct(a.shape, BF16))
            out_specs.append(pl.BlockSpec((r, a.shape[1]), step))
    for (rows, cols), r in zip(zero_jobs, zero_rows):
        if r is not None:
            out_shape.append(jax.ShapeDtypeStruct((rows, cols), F32))
            out_specs.append(pl.BlockSpec((r, cols), step))

    outs = pl.pallas_call(
        functools.partial(_attn_kernel, tk=tk, kv_chunks=tuple(n // tk for n in kv_lens),
                          scale=HEAD_DIM ** -0.5, n_cast=len(fused_cast), n_zero=len(fused_zero)),
        out_shape=out_shape,
        grid=(b, N_KV_HEADS, nq),
        in_specs=in_specs,
        out_specs=out_specs,
        compiler_params=_cparams(("arbitrary", "arbitrary", "arbitrary")),
        name="attention",
    )(q3, *[a for kv in kv_sources for a in kv], *fused_cast)

    it = iter(outs[1:])
    casts = [next(it) if r is not None else a.astype(BF16) for a, r in zip(cast_jobs, cast_rows)]
    zeros = [next(it) if r is not None else jnp.zeros(z, F32) for z, r in zip(zero_jobs, zero_rows)]
    return outs[0], casts, zeros


def _pool_kernel(up_ref, u_ref, un_ref, wp_ref, ps_ref, o_ref, ext_ref, *, tp, seq):
    i = pl.program_id(0)
    pos0 = (i * tp) % seq
    has_prev = pos0 > 0
    has_next = pos0 + tp < seq
    h = POOL_HALO
    ext_ref[0:h, :] = jnp.where(has_prev, up_ref[...], 0.0)
    ext_ref[h:h + tp, :] = u_ref[...]
    ext_ref[h + tp:h + tp + h, :] = jnp.where(has_next, un_ref[...], 0.0)

    gdim = wp_ref.shape[1]
    pos = pos0 + lax.broadcasted_iota(I32, (tp, 1), 0)
    for g, w in enumerate(POOL_WINDOWS):
        c0 = g * gdim
        acc = jnp.zeros((tp, gdim), F32)
        for j in range(-(w // 2), w - w // 2):
            acc = acc + ext_ref[h + j:h + j + tp, c0:c0 + gdim]
        lo = jnp.maximum(pos - w // 2, 0)
        hi = jnp.minimum(pos + (w - w // 2), seq)
        mean = acc / (hi - lo).astype(F32)
        pooled = (mean - u_ref[:, c0:c0 + gdim]).astype(BF16)
        y = jnp.dot(pooled, wp_ref[g], preferred_element_type=F32) * ps_ref[:, c0:c0 + gdim]
        o_ref[:, c0:c0 + gdim] = y.astype(o_ref.dtype)


def _pool_mixer(u2d, seq, w_pool_b, pool_scale):
    n, d_pool = u2d.shape
    tp = _pick(seq, (512, 256, 128))
    h = POOL_HALO
    assert max(POOL_WINDOWS) // 2 <= h
    nb = n // h
    per = tp // h
    return pl.pallas_call(
        functools.partial(_pool_kernel, tp=tp, seq=seq),
        out_shape=jax.ShapeDtypeStruct((n, d_pool), BF16),
        grid=(n // tp,),
        in_specs=[
            pl.BlockSpec((h, d_pool), lambda i: (jnp.maximum(i * per - 1, 0), 0)),
            pl.BlockSpec((tp, d_pool), lambda i: (i, 0)),
            pl.BlockSpec((h, d_pool), lambda i: (jnp.minimum((i + 1) * per, nb - 1), 0)),
            _resident(w_pool_b.shape, lambda i: (0, 0, 0)),
            _resident((1, d_pool), lambda i: (0, 0)),
        ],
        out_specs=pl.BlockSpec((tp, d_pool), lambda i: (i, 0)),
        scratch_shapes=[pltpu.VMEM((tp + 2 * h, d_pool), F32)],
        compiler_params=_cparams(("arbitrary",)),
        name="pool_mixer",
    )(u2d, u2d, u2d, w_pool_b, pool_scale.reshape(1, d_pool))


def _merge_kernel(h_ref, pm_ref, at_ref, wga_ref, wgb_ref, bga_ref, bgb_ref, wa_ref, wb_ref, o_ref):
    hb = h_ref[...]
    ga = jax.nn.sigmoid(jnp.dot(hb, wga_ref[...], preferred_element_type=F32) + bga_ref[...])
    gb = jax.nn.sigmoid(jnp.dot(hb, wgb_ref[...], preferred_element_type=F32) + bgb_ref[...])
    ya = jnp.dot(pm_ref[...], wa_ref[...], preferred_element_type=F32)
    yb = jnp.dot(at_ref[...], wb_ref[...], preferred_element_type=F32)
    o_ref[...] = (ga * ya + gb * yb).astype(o_ref.dtype)


def _gated_merge(h2d, pm2d, at2d, w_gate_b, b_gate, w_a_b, w_b_b):
    n, d = h2d.shape
    d_pool = pm2d.shape[1]
    d_q = at2d.shape[1]
    tm = _pick(n, (1024, 512, 256, 128))
    tn = _pick(d, (512, 256, 128))
    nj = d // tn
    bg = b_gate.reshape(1, 2 * d)
    return pl.pallas_call(
        _merge_kernel,
        out_shape=jax.ShapeDtypeStruct((n, d), BF16),
        grid=(n // tm, nj),
        in_specs=[
            pl.BlockSpec((tm, d), lambda i, j: (i, 0)),
            pl.BlockSpec((tm, d_pool), lambda i, j: (i, 0)),
            pl.BlockSpec((tm, d_q), lambda i, j: (i, 0)),
            pl.BlockSpec((d, tn), lambda i, j: (0, j)),
            pl.BlockSpec((d, tn), lambda i, j: (0, nj + j)),
            pl.BlockSpec((1, tn), lambda i, j: (0, j)),
            pl.BlockSpec((1, tn), lambda i, j: (0, nj + j)),
            pl.BlockSpec((d_pool, tn), lambda i, j: (0, j)),
            pl.BlockSpec((d_q, tn), lambda i, j: (0, j)),
        ],
        out_specs=pl.BlockSpec((tm, tn), lambda i, j: (i, j)),
        compiler_params=_cparams(("arbitrary", "arbitrary")),
        name="gated_merge",
    )(h2d, pm2d, at2d, w_gate_b, w_gate_b, bg, bg, w_a_b, w_b_b)


def _outproj_kernel(*refs):
    cnt0_ref, cnt_ref = refs[7], refs[13]

    @pl.when(pl.program_id(0) == 0)
    def _():
        cnt_ref[...] = cnt0_ref[...]

    row_tiled = [True, True] + [False] * 6 + [True] * 5 + [False]
    rows = refs[0].shape[0] // OUT_PROJ_SUB_BLOCKS
    for sb in range(OUT_PROJ_SUB_BLOCKS):
        _outproj_rows(*[r.at[pl.ds(sb * rows, rows)] if t else r for r, t in zip(refs, row_tiled)])


def _outproj_rows(mg_ref, x_ref, mod_ref, wo_ref, g2_ref, wr_ref, br_ref, cnt0_ref,
                  x1_ref, h2_ref, idx_ref, gw_ref, rk_ref, cnt_ref):
    del cnt0_ref
    mix = jnp.dot(mg_ref[...], wo_ref[...], preferred_element_type=F32)
    gate1 = mod_ref[0, 2:3, :]
    shift2 = mod_ref[0, 3:4, :]
    scale2 = mod_ref[0, 4:5, :]
    x1 = x_ref[...] + gate1 * mix
    x1_ref[...] = x1
    ms = jnp.mean(x1 * x1, axis=-1, keepdims=True)
    h2 = (x1 * lax.rsqrt(ms + EPS) * g2_ref[...]) * (1.0 + scale2) + shift2
    h2_ref[...] = h2
    logits = jnp.dot(h2.astype(BF16), wr_ref[...], preferred_element_type=F32) + br_ref[...]
    idx, gw, rk, tile_counts = _route_tile(logits, cnt_ref[...])
    idx_ref[...] = idx
    gw_ref[...] = gw
    rk_ref[...] = rk
    cnt_ref[...] += tile_counts


def _out_projection(mg2d, x2d, mod3, mod_row, w_o_b, g2, w_router, b_router, counts0):
    n, d = x2d.shape
    ne = w_router.shape[1]
    tm = _pick(n, OUT_PROJ_TILES)
    row_spec = pl.BlockSpec((tm, d), lambda i: (i, 0))
    k_spec = pl.BlockSpec((tm, TOP_K), lambda i: (i, 0))
    return pl.pallas_call(
        _outproj_kernel,
        out_shape=[
            jax.ShapeDtypeStruct((n, d), F32),
            jax.ShapeDtypeStruct((n, d), F32),
            jax.ShapeDtypeStruct((n, TOP_K), I32),
            jax.ShapeDtypeStruct((n, TOP_K), F32),
            jax.ShapeDtypeStruct((n, TOP_K), I32),
            jax.ShapeDtypeStruct((1, ne), F32),
        ],
        grid=(n // tm,),
        in_specs=[
            row_spec,
            row_spec,
            pl.BlockSpec((1, N_MOD, d), lambda i: (mod_row(i * tm), 0, 0)),
            _resident((d, d), lambda i: (0, 0)),
            _resident((1, d), lambda i: (0, 0)),
            _resident((d, ne), lambda i: (0, 0)),
            _resident((1, ne), lambda i: (0, 0)),
            _resident((1, ne), lambda i: (0, 0)),
        ],
        out_specs=[row_spec, row_spec, k_spec, k_spec, k_spec, pl.BlockSpec((1, ne), lambda i: (0, 0))],
        compiler_params=_cparams(("arbitrary",)),
        name="out_projection",
    )(mg2d, x2d, mod3, w_o_b, g2.reshape(1, d), w_router, b_router.reshape(1, ne), counts0)


def _route_tile(lg, count_before):
    tr, ne = lg.shape
    lane = lax.broadcasted_iota(I32, (tr, ne), 1)
    work = lg
    vals, idxs = [], []
    member = jnp.zeros((tr, ne), F32)
    for _ in range(TOP_K):
        mx = jnp.max(work, axis=-1, keepdims=True)
        ix = jnp.min(jnp.where(work == mx, lane, ne), axis=-1, keepdims=True)
        sel = lane == ix
        vals.append(mx)
        idxs.append(ix)
        member = jnp.where(sel, 1.0, member)
        work = jnp.where(sel, -jnp.inf, work)

    ex = [jnp.exp(v - vals[0]) for v in vals]
    den = ex[0]
    for e in ex[1:]:
        den = den + e

    row = lax.broadcasted_iota(I32, (tr, tr), 0)
    col = lax.broadcasted_iota(I32, (tr, tr), 1)
    lower = jnp.where(col < row, 1.0, 0.0).astype(BF16)
    before = jnp.dot(lower, member.astype(BF16), preferred_element_type=F32) + count_before

    k_lane = lax.broadcasted_iota(I32, (tr, TOP_K), 1)
    idx_out = jnp.zeros((tr, TOP_K), I32)
    gw_out = jnp.zeros((tr, TOP_K), F32)
    rk_out = jnp.zeros((tr, TOP_K), I32)
    for k in range(TOP_K):
        rk = jnp.sum(jnp.where(lane == idxs[k], before, 0.0), axis=-1, keepdims=True).astype(I32)
        idx_out = jnp.where(k_lane == k, idxs[k], idx_out)
        gw_out = jnp.where(k_lane == k, ex[k] / den, gw_out)
        rk_out = jnp.where(k_lane == k, rk, rk_out)
    return idx_out, gw_out, rk_out, jnp.sum(member, axis=0, keepdims=True)


def _dispatch_kernel(dest_ref, h_ref, xs_in_hbm, xs_hbm, sem, *, td):
    del xs_in_hbm

    def start(r, carry):
        for k in range(TOP_K):
            d = dest_ref[0, 0, r * TOP_K + k]
            pltpu.make_async_copy(h_ref.at[pl.ds(r, 1)], xs_hbm.at[pl.ds(d, 1)], sem).start()
        return carry

    lax.fori_loop(0, td, start, 0)
    for _ in range(TOP_K):
        pltpu.make_async_copy(h_ref, xs_hbm.at[pl.ds(0, td)], sem).wait()


def _dispatch(h2d, dest, row0, xs0):
    n, d = h2d.shape
    n_slots = xs0.shape[0]
    td = _pick(math.gcd(n, row0) if row0 else n, (512, 256, 128))
    assert dest.shape[0] % td == 0
    dest3 = dest.reshape(dest.shape[0] // td, 1, td * TOP_K)
    blk0 = row0 // td
    return pl.pallas_call(
        functools.partial(_dispatch_kernel, td=td),
        out_shape=jax.ShapeDtypeStruct((n_slots, d), h2d.dtype),
        grid=(n // td,),
        in_specs=[
            pl.BlockSpec((1, 1, td * TOP_K), lambda i: (blk0 + i, 0, 0), memory_space=pltpu.SMEM),
            pl.BlockSpec((td, d), lambda i: (i, 0)),
            pl.BlockSpec(memory_space=pl.ANY),
        ],
        out_specs=pl.BlockSpec(memory_space=pl.ANY),
        scratch_shapes=[pltpu.SemaphoreType.DMA(())],
        input_output_aliases={2: 0},
        compiler_params=_cparams(("arbitrary",)),
        name="dispatch",
    )(dest3, h2d, xs0)


def _expert_kernel(te_ref, tr_ref, nu_ref, x_ref, wg_ref, wl_ref, bg_ref, bl_ref, wd_ref, bd_ref, y_ref, *, nf):
    del te_ref
    i = pl.program_id(0)
    f = pl.program_id(1)
    used = i < nu_ref[0]
    rows = tr_ref[jnp.minimum(i, nu_ref[0] - 1)]
    te = x_ref.shape[0]
    half = te // 2

    def mlp(m, first):
        xb = x_ref[0:m, :].astype(BF16)
        g = jnp.dot(xb, wg_ref[0], preferred_element_type=F32) + bg_ref[0]
        lin = jnp.dot(xb, wl_ref[0], preferred_element_type=F32) + bl_ref[0]
        g = jnp.minimum(g, SWIGLU_LIMIT)
        lin = jnp.clip(lin, -SWIGLU_LIMIT, SWIGLU_LIMIT)
        act = g * jax.nn.sigmoid(SWIGLU_ALPHA * g) * (lin + 1.0)
        down = jnp.dot(act.astype(BF16), wd_ref[0], preferred_element_type=F32)
        if first:
            y_ref[0:m, :] = down + bd_ref[0]
            if m < te:
                y_ref[m:te, :] = jnp.zeros((te - m, y_ref.shape[1]), F32)
        else:
            y_ref[0:m, :] += down

    for m, occupied in ((te, rows > half), (half, rows <= half)):
        for first in (True, False):
            at_step = (f == 0) if first else (f != 0)

            @pl.when(jnp.logical_and(jnp.logical_and(used, occupied), at_step))
            def _():
                mlp(m, first)

    @pl.when(jnp.logical_and(jnp.logical_not(used), f == nf - 1))
    def _():
        y_ref[...] = jnp.zeros_like(y_ref)


def _expert_mlp(xs, tile_expert, tile_rows, n_used, w_gu_b, b_gu, w_down_b, b_down, te):
    n_slots, d = xs.shape
    ne, _, d_ff2 = w_gu_b.shape
    d_ff = d_ff2 // 2
    tf = _pick(d_ff, (1024, 512, 256, 128))
    nf = d_ff // tf
    n_tiles = n_slots // te

    def tile(i, nu):
        return jnp.minimum(i, nu[0] - 1)

    def ff(i, f, nu):
        return jnp.where(i < nu[0], f, nf - 1)

    grid_spec = pltpu.PrefetchScalarGridSpec(
        num_scalar_prefetch=3,
        grid=(n_tiles, nf),
        in_specs=[
            pl.BlockSpec((te, d), lambda i, f, tx, tr, nu: (tile(i, nu), 0)),
            pl.BlockSpec((1, d, tf), lambda i, f, tx, tr, nu: (tx[tile(i, nu)], 0, ff(i, f, nu))),
            pl.BlockSpec((1, d, tf), lambda i, f, tx, tr, nu: (tx[tile(i, nu)], 0, nf + ff(i, f, nu))),
            pl.BlockSpec((1, 1, tf), lambda i, f, tx, tr, nu: (tx[tile(i, nu)], 0, ff(i, f, nu))),
            pl.BlockSpec((1, 1, tf), lambda i, f, tx, tr, nu: (tx[tile(i, nu)], 0, nf + ff(i, f, nu))),
            pl.BlockSpec((1, tf, d), lambda i, f, tx, tr, nu: (tx[tile(i, nu)], ff(i, f, nu), 0)),
            pl.BlockSpec((1, 1, d), lambda i, f, tx, tr, nu: (tx[tile(i, nu)], 0, 0)),
        ],
        out_specs=pl.BlockSpec((te, d), lambda i, f, tx, tr, nu: (i, 0)),
    )
    return pl.pallas_call(
        functools.partial(_expert_kernel, nf=nf),
        out_shape=jax.ShapeDtypeStruct((n_slots, d), F32),
        grid_spec=grid_spec,
        compiler_params=_cparams(("arbitrary", "arbitrary")),
        name="expert_mlp",
    )(tile_expert, tile_rows, n_used, xs, w_gu_b, w_gu_b, b_gu.reshape(ne, 1, d_ff2), b_gu.reshape(ne, 1, d_ff2),
      w_down_b, b_down.reshape(ne, 1, d))


def _combine_kernel(dcur_ref, dnext_ref, x1_ref, gw_ref, mod_ref, gf_ref, y_hbm, o_ref, ybuf, sems, *,
                    tc, n_steps):
    i = pl.program_id(0)
    slot = i % 2

    def issue(dest_ref, s):
        def start(r, carry):
            for k in range(TOP_K):
                d = dest_ref[0, 0, r * TOP_K + k]
                pltpu.make_async_copy(y_hbm.at[pl.ds(d, 1)], ybuf.at[s, k, pl.ds(r, 1)], sems.at[s]).start()
            return carry

        lax.fori_loop(0, tc, start, 0)

    @pl.when(i == 0)
    def _():
        issue(dcur_ref, 0)

    for s in range(2):
        @pl.when(jnp.logical_and(i + 1 < n_steps, slot == 1 - s))
        def _():
            issue(dnext_ref, s)

    for k in range(TOP_K):
        pltpu.make_async_copy(y_hbm.at[pl.ds(0, tc)], ybuf.at[slot, k], sems.at[slot]).wait()

    gw = gw_ref[...]
    moe = gw[:, 0:1] * ybuf[slot, 0]
    for k in range(1, TOP_K):
        moe = moe + gw[:, k:k + 1] * ybuf[slot, k]
    gate2 = mod_ref[0, 5:6, :]
    x2 = x1_ref[...] + gate2 * moe
    ms = jnp.mean(x2 * x2, axis=-1, keepdims=True)
    o_ref[...] = x2 * lax.rsqrt(ms + EPS) * gf_ref[...]


def _combine(x1, gate_w, dest, row0, y, mod3, mod_row, final_g):
    n, d = x1.shape
    tc = _pick(math.gcd(n, row0) if row0 else n, (256, 128))
    assert dest.shape[0] % tc == 0
    n_steps = n // tc
    dest3 = dest.reshape(dest.shape[0] // tc, 1, tc * TOP_K)
    blk0 = row0 // tc
    return pl.pallas_call(
        functools.partial(_combine_kernel, tc=tc, n_steps=n_steps),
        out_shape=jax.ShapeDtypeStruct((n, d), F32),
        grid=(n_steps,),
        in_specs=[
            pl.BlockSpec((1, 1, tc * TOP_K), lambda i: (blk0 + i, 0, 0), memory_space=pltpu.SMEM),
            pl.BlockSpec((1, 1, tc * TOP_K), lambda i: (blk0 + jnp.minimum(i + 1, n_steps - 1), 0, 0),
                         memory_space=pltpu.SMEM),
            pl.BlockSpec((tc, d), lambda i: (i, 0)),
            pl.BlockSpec((tc, TOP_K), lambda i: (blk0 + i, 0)),
            pl.BlockSpec((1, N_MOD, d), lambda i: (mod_row(i * tc), 0, 0)),
            _resident((1, d), lambda i: (0, 0)),
            pl.BlockSpec(memory_space=pl.ANY),
        ],
        out_specs=pl.BlockSpec((tc, d), lambda i: (i, 0)),
        scratch_shapes=[pltpu.VMEM((2, TOP_K, tc, d), F32), pltpu.SemaphoreType.DMA((2,))],
        compiler_params=_cparams(("arbitrary",)),
        name="combine",
    )(dest3, dest3, x1, gate_w, mod3, final_g.reshape(1, d), y)


def _rope_tables(seq):
    rows = seq // GRID_W
    row = jnp.repeat(jnp.arange(rows), GRID_W).astype(F32)
    col = jnp.tile(jnp.arange(GRID_W), rows).astype(F32)
    freqs = ROPE_THETA ** (-jnp.arange(ROPE_PAIRS_PER_AXIS, dtype=F32) / ROPE_PAIRS_PER_AXIS)
    ang = jnp.concatenate([row[:, None] * freqs, col[:, None] * freqs], axis=-1)
    cos, sin = jnp.cos(ang), jnp.sin(ang)
    return jnp.concatenate([cos, cos], axis=-1), jnp.concatenate([-sin, sin], axis=-1)


def _slot_tiles(n_tokens, n_experts):
    return (n_tokens * TOP_K) // EXPERT_TILE_ROWS + n_experts


def _moe(groups, counts, wts, final_g, mod3, xs0):
    te = EXPERT_TILE_ROWS
    sizes = [g[0].shape[0] for g in groups]
    n = sum(sizes)
    top_i, gate_w, rank = (jnp.concatenate([g[j] for g in groups], axis=0) for j in (2, 3, 4))
    ne = counts.shape[1]

    counts = counts.reshape(ne).astype(I32)
    padded = (counts + te - 1) // te * te
    pad_end = jnp.cumsum(padded)
    pad_start = pad_end - padded
    n_tiles = _slot_tiles(n, ne)
    assert xs0.shape[0] == n_tiles * te
    tile_start = jnp.arange(n_tiles, dtype=I32) * te
    tile_expert = jnp.minimum(jnp.sum((pad_end[None, :] <= tile_start[:, None]).astype(I32), axis=1), ne - 1)
    tile_rows = jnp.clip(counts[tile_expert] - (tile_start - pad_start[tile_expert]), 0, te).astype(I32)
    n_used = (pad_end[-1:] // te).astype(I32)
    onehot = top_i[:, :, None] == jnp.arange(ne, dtype=I32)
    dest = jnp.sum(jnp.where(onehot, pad_start, 0), axis=-1) + rank

    offs = [sum(sizes[:i]) for i in range(len(sizes))]
    xs = xs0
    for g, o in zip(groups, offs):
        xs = _dispatch(g[1], dest, o, xs)
    y = _expert_mlp(xs, tile_expert, tile_rows, n_used, wts["w_gu"], wts["b_gu"], wts["w_down"], wts["b_down"], te)
    return [_combine(g[0], gate_w, dest, o, y, mod3, g[5], final_g) for g, o in zip(groups, offs)]


def _mixer(x2d, batch, seq, mod3, mod_row, rope, kv_ctx, wts, counts0, emit_kv_f32, cast_jobs=(), zero_jobs=()):
    outs = _in_projection(x2d, seq, mod3, mod_row, wts["norm1_g"], wts["w_in"], wts["q_norm_g"],
                          wts["k_norm_g"], rope, emit_kv_f32)
    h, u, q, k, v = outs[:5]
    d_q = q.shape[1]
    d_kv = k.shape[1]
    q3 = q.reshape(batch, seq, d_q)
    k3 = k.reshape(batch, seq, d_kv)
    v3 = v.reshape(batch, seq, d_kv)
    kv_sources = ([kv_ctx] if kv_ctx is not None else []) + [(k3, v3)]
    attn, casts, zeros = _attention(q3, kv_sources, cast_jobs, zero_jobs)
    attn = attn.reshape(batch * seq, d_q)
    pm = _pool_mixer(u, seq, wts["w_pool"], wts["pool_scale"])
    merged = _gated_merge(h, pm, attn, wts["w_gate"], wts["b_gate"], wts["w_a_out"], wts["w_b_out"])
    x1, h2, top_i, gate_w, rank, counts = _out_projection(
        merged, x2d, mod3, mod_row, wts["w_o"], wts["norm2_g"], wts["w_router"], wts["b_router"], counts0)
    return (x1, h2, top_i, gate_w, rank, mod_row), counts, outs[5:], casts, zeros


def kernel(x_prompt, x_sample, cache_k, cache_v, c, c_ctx, w_mod, b_mod, norm1_g, w_in, q_norm_g, k_norm_g, w_pool, pool_scale, w_a_out, w_b_out, w_gate, b_gate, w_o, norm2_g, w_router, b_router, w_gu, b_gu, w_down, b_down, final_g):
    depth = w_mod.shape[0]
    assert depth == 1, "the final norm is fused into the last combine; one trunk layer is supported"
    bp, sp, d = x_prompt.shape
    bs, ss, _ = x_sample.shape
    l = 0

    n_rows = -(-(bs + 1) // SUBLANES) * SUBLANES
    cvec = jnp.zeros((n_rows, d), F32).at[:bs].set(c).at[bs].set(c_ctx)
    mod3 = _modulation(cvec, w_mod[l], b_mod[l]).reshape(n_rows, N_MOD, d)

    wts = dict(
        norm1_g=norm1_g[l], w_in=w_in[l].astype(BF16), q_norm_g=q_norm_g[l], k_norm_g=k_norm_g[l],
        w_pool=w_pool[l].astype(BF16), pool_scale=pool_scale[l], w_a_out=w_a_out[l].astype(BF16),
        w_b_out=w_b_out[l].astype(BF16), w_gate=w_gate[l].astype(BF16), b_gate=b_gate[l],
        w_o=w_o[l].astype(BF16), norm2_g=norm2_g[l], w_router=w_router[l].astype(BF16), b_router=b_router[l],
        b_gu=b_gu[l], b_down=b_down[l],
    )

    ne, _, d_ff2 = w_gu[l].shape
    gp, counts_p, (kf, vf), _, _ = _mixer(x_prompt.reshape(bp * sp, d), bp, sp, mod3, lambda t: bs, None, None,
                                          wts, jnp.zeros((1, ne), F32), True)
    kv_ctx = (cache_k[:, l].reshape(bs, -1, N_KV_HEADS * HEAD_DIM).astype(BF16),
              cache_v[:, l].reshape(bs, -1, N_KV_HEADS * HEAD_DIM).astype(BF16))
    n_slots = _slot_tiles(bp * sp + bs * ss, ne) * EXPERT_TILE_ROWS
    gs, counts, _, (w_gu_b, w_down_b), (xs0,) = _mixer(
        x_sample.reshape(bs * ss, d), bs, ss, mod3, lambda t: t // ss, _rope_tables(ss), kv_ctx, wts, counts_p, False,
        cast_jobs=(w_gu[l].reshape(ne * d, d_ff2), w_down[l].reshape(-1, d)), zero_jobs=((n_slots, d),))
    wts["w_gu"] = w_gu_b.reshape(ne, d, d_ff2)
    wts["w_down"] = w_down_b.reshape(w_down[l].shape)
    yp, ys = _moe([gp, gs], counts, wts, final_g, mod3, xs0)

    y_prompt = yp.reshape(bp, sp, d)
    y_sample = ys.reshape(bs, ss, d)
    new_cache_k = kf.reshape(bp, 1, sp, N_KV_HEADS, HEAD_DIM)
    new_cache_v = vf.reshape(bp, 1, sp, N_KV_HEADS, HEAD_DIM)
    return (y_prompt, y_sample, new_cache_k, new_cache_v)
```
